```python
import jax, jax.numpy as jnp
from jax import lax
import numpy as np

D_MODEL = 2048
BATCH = 4
SEQ = 2048
DEPTH = 2

N_A = DEPTH // 2
N_B = DEPTH - N_A
CONV_WIDTH = 31
N_HEADS = 16
QK_NOPE_DIM = 128
QK_ROPE_DIM = 64
V_HEAD_DIM = 128
KV_LORA_RANK = 512
Q_LORA_RANK = 512
ROPE_THETA = 10000.0
Q_BLOCK = 128
D_FF = 4 * D_MODEL
EPS = 1e-6
NEG = -1e30

kernel_name = "yoco_conformer_mla_hybrid"


def rms_norm(x, g):
    xf = x.astype(jnp.float32)
    y = xf * lax.rsqrt(jnp.mean(xf * xf, axis=-1, keepdims=True) + EPS)
    return (y * g.astype(jnp.float32)).astype(x.dtype)


def layer_norm(x, g, b):
    xf = x.astype(jnp.float32)
    mu = jnp.mean(xf, axis=-1, keepdims=True)
    var = jnp.mean(jnp.square(xf - mu), axis=-1, keepdims=True)
    y = (xf - mu) * lax.rsqrt(var + EPS)
    return (y * g.astype(jnp.float32) + b.astype(jnp.float32)).astype(x.dtype)


def adaln(c, w, b, n):
    m = jax.nn.silu(c) @ w + b
    return jnp.split(m, n, axis=-1)


def modulate(h, shift, scale):
    return h * (1.0 + scale[:, None, :]) + shift[:, None, :]


def rope_tables(positions, dtype):
    inv = 1.0 / (ROPE_THETA ** (jnp.arange(0, QK_ROPE_DIM, 2, dtype=jnp.float32) / QK_ROPE_DIM))
    ang = positions.astype(jnp.float32)[..., None] * inv
    return jnp.cos(ang).astype(dtype), jnp.sin(ang).astype(dtype)


def apply_rope(x, cos, sin):
    x1, x2 = jnp.split(x, 2, axis=-1)
    return jnp.concatenate([x1 * cos - x2 * sin, x2 * cos + x1 * sin], axis=-1)


def conformer_conv(h, w_in, b_in, dw, dw_b, ln_g, ln_b, w_out, b_out):
    u = jax.nn.glu(h @ w_in + b_in, axis=-1)
    u = lax.conv_general_dilated(
        u, dw[:, None, :], window_strides=(1,), padding=[(CONV_WIDTH - 1, 0)],
        dimension_numbers=("NWC", "WIO", "NWC"), feature_group_count=D_MODEL) + dw_b
    u = jax.nn.silu(layer_norm(u, ln_g, ln_b))
    return u @ w_out + b_out


def mla_attention(q_nope, q_rope, k_nope, k_rope, v):
    b, s, h, _ = q_nope.shape
    nblk = s // Q_BLOCK
    scale = (QK_NOPE_DIM + QK_ROPE_DIM) ** -0.5
    qn = q_nope.reshape(b, nblk, Q_BLOCK, h, QK_NOPE_DIM).transpose(1, 0, 2, 3, 4)
    qr = q_rope.reshape(b, nblk, Q_BLOCK, h, QK_ROPE_DIM).transpose(1, 0, 2, 3, 4)
    key_pos = jnp.arange(s)

    def one_block(args):
        i, qn_b, qr_b = args
        sc = (jnp.einsum("bqhd,bkhd->bhqk", qn_b, k_nope)
              + jnp.einsum("bqhd,bkd->bhqk", qr_b, k_rope)).astype(jnp.float32) * scale
        q_pos = i * Q_BLOCK + jnp.arange(Q_BLOCK)
        mask = key_pos[None, :] <= q_pos[:, None]
        sc = jnp.where(mask[None, None], sc, NEG)
        p = jax.nn.softmax(sc, axis=-1).astype(v.dtype)
        return jnp.einsum("bhqk,bkhd->bqhd", p, v)

    out = lax.map(one_block, (jnp.arange(nblk), qn, qr))
    return out.transpose(1, 0, 2, 3, 4).reshape(b, s, h * V_HEAD_DIM)


def setup_inputs(seed: int = 0) -> dict:
    key = jax.random.key(seed)
    ks = iter(jax.random.split(key, 64))
    D = D_MODEL

    def nrm(shape, scale):
        return jax.random.normal(next(ks), shape, jnp.float32) * scale

    def gain(shape):
        return 1.0 + nrm(shape, 0.02)

    x = jax.random.normal(next(ks), (BATCH, SEQ, D), jnp.float32)
    c = jax.random.normal(next(ks), (BATCH, D), jnp.float32)
    positions = (jnp.arange(SEQ, dtype=jnp.int32)[None, :]
                 + jax.random.randint(next(ks), (BATCH, 1), 0, 1024, dtype=jnp.int32))
    ada_s = 0.1 * D ** -0.5
    return {
        "x": x, "c": c, "positions": positions,
        "w_ada_mix": nrm((DEPTH, D, 3 * D), ada_s), "b_ada_mix": nrm((DEPTH, 3 * D), 0.01),
        "w_ada_mlp": nrm((DEPTH, D, 3 * D), ada_s), "b_ada_mlp": nrm((DEPTH, 3 * D), 0.01),
        "g_pre_mix": gain((DEPTH, D)), "g_post_mix": gain((DEPTH, D)),
        "g_pre_mlp": gain((DEPTH, D)), "g_post_mlp": gain((DEPTH, D)),
        "conv_w_in": nrm((N_A, D, 2 * D), D ** -0.5), "conv_b_in": nrm((N_A, 2 * D), 0.01),
        "conv_dw": nrm((N_A, CONV_WIDTH, D), CONV_WIDTH ** -0.5), "conv_dw_b": nrm((N_A, D), 0.01),
        "conv_ln_g": gain((N_A, D)), "conv_ln_b": nrm((N_A, D), 0.01),
        "conv_w_out": nrm((N_A, D, D), D ** -0.5), "conv_b_out": nrm((N_A, D), 0.01),
        "w_ada_kv": nrm((D, 2 * D), ada_s), "b_ada_kv": nrm((2 * D,), 0.01), "g_kv": gain((D,)),
        "w_dkv": nrm((D, KV_LORA_RANK), D ** -0.5), "g_ckv": gain((KV_LORA_RANK,)),
        "w_kr": nrm((D, QK_ROPE_DIM), D ** -0.5),
        "w_uk": nrm((KV_LORA_RANK, N_HEADS * QK_NOPE_DIM), KV_LORA_RANK ** -0.5),
        "w_uv": nrm((KV_LORA_RANK, N_HEADS * V_HEAD_DIM), KV_LORA_RANK ** -0.5),
        "w_dq": nrm((N_B, D, Q_LORA_RANK), D ** -0.5), "g_cq": gain((N_B, Q_LORA_RANK)),
        "w_uq": nrm((N_B, Q_LORA_RANK, N_HEADS * (QK_NOPE_DIM + QK_ROPE_DIM)), Q_LORA_RANK ** -0.5),
        "w_o": nrm((N_B, N_HEADS * V_HEAD_DIM, D), (N_HEADS * V_HEAD_DIM) ** -0.5),
        "mlp_w_up": nrm((DEPTH, D, D_FF), D ** -0.5),
        "mlp_w_down": nrm((DEPTH, D_FF, D), D_FF ** -0.5),
    }


def reference(x, c, positions,
              w_ada_mix, b_ada_mix, w_ada_mlp, b_ada_mlp,
              g_pre_mix, g_post_mix, g_pre_mlp, g_post_mlp,
              conv_w_in, conv_b_in, conv_dw, conv_dw_b, conv_ln_g, conv_ln_b, conv_w_out, conv_b_out,
              w_ada_kv, b_ada_kv, g_kv,
              w_dkv, g_ckv, w_kr, w_uk, w_uv,
              w_dq, g_cq, w_uq, w_o,
              mlp_w_up, mlp_w_down):
    b, s, _ = x.shape
    cos, sin = rope_tables(positions, x.dtype)
    h = x
    k_nope = k_rope = v = None
    for l in range(DEPTH):
        shift, scale, gate = adaln(c, w_ada_mix[l], b_ada_mix[l], 3)
        hn = modulate(rms_norm(h, g_pre_mix[l]), shift, scale)
        if l < N_A:
            y = conformer_conv(hn, conv_w_in[l], conv_b_in[l], conv_dw[l], conv_dw_b[l],
                               conv_ln_g[l], conv_ln_b[l], conv_w_out[l], conv_b_out[l])
        else:
            j = l - N_A
            if j == 0:
                kv_shift, kv_scale = adaln(c, w_ada_kv, b_ada_kv, 2)
                kvn = modulate(rms_norm(h, g_kv), kv_shift, kv_scale)
                c_kv = rms_norm(kvn @ w_dkv, g_ckv)
                k_nope = (c_kv @ w_uk).reshape(b, s, N_HEADS, QK_NOPE_DIM)
                v = (c_kv @ w_uv).reshape(b, s, N_HEADS, V_HEAD_DIM)
                k_rope = apply_rope(kvn @ w_kr, cos, sin)
            c_q = rms_norm(hn @ w_dq[j], g_cq[j])
            q = (c_q @ w_uq[j]).reshape(b, s, N_HEADS, QK_NOPE_DIM + QK_ROPE_DIM)
            q_nope = q[..., :QK_NOPE_DIM]
            q_rope = apply_rope(q[..., QK_NOPE_DIM:], cos[:, :, None, :], sin[:, :, None, :])
            y = mla_attention(q_nope, q_rope, k_nope, k_rope, v) @ w_o[j]
        h = h + gate[:, None, :] * rms_norm(y, g_post_mix[l])
        shift, scale, gate = adaln(c, w_ada_mlp[l], b_ada_mlp[l], 3)
        hn = modulate(rms_norm(h, g_pre_mlp[l]), shift, scale)
        y = jnp.square(jax.nn.relu(hn @ mlp_w_up[l])) @ mlp_w_down[l]
        h = h + gate[:, None, :] * rms_norm(y, g_post_mlp[l])
    return h
```

```python
import functools

import jax
import jax.numpy as jnp
from jax import lax
from jax.experimental import pallas as pl
from jax.experimental.pallas import tpu as pltpu

F32 = jnp.float32
BF16 = jnp.bfloat16

EPS = 1e-6
NEG = -1e30
ROPE_THETA = 10000.0

N_HEADS = 16
QK_NOPE_DIM = 128
QK_ROPE_DIM = 64
QK_DIM = QK_NOPE_DIM + QK_ROPE_DIM
V_HEAD_DIM = 128
CONV_WIDTH = 31

LANES = 128
SUBLANES = 8
VMEM_BYTES_V7X = 64 * 1024 * 1024
VMEM_LIMIT = VMEM_BYTES_V7X - 8 * 1024 * 1024

ROW_CHUNK = 16
CONV_ROWS = 32
CONV_LANES = 512
HALO = 32


def _cparams(*sem):
    return pltpu.CompilerParams(dimension_semantics=sem, vmem_limit_bytes=VMEM_LIMIT)


def _dot(a, b):
    return jnp.dot(a, b, preferred_element_type=F32)


def _sigmoid(x):
    return 1.0 / (1.0 + jnp.exp(-x))


def _row_loop(n_rows, body):
    def step(i, carry):
        r0 = pl.multiple_of(i * ROW_CHUNK, ROW_CHUNK)
        body(pl.ds(r0, ROW_CHUNK))
        return carry
    lax.fori_loop(0, n_rows // ROW_CHUNK, step, 0)


def _inv_rms(x):
    return lax.rsqrt(jnp.mean(x * x, axis=-1, keepdims=True) + EPS)


def _ada_kernel(c_ref, w_ref, b_ref, o_ref):
    c = c_ref[...]
    sc = (c * _sigmoid(c)).astype(BF16)
    o_ref[0] = _dot(sc, w_ref[0].astype(BF16)) + b_ref[0]


def _ada(c_pad, w, b, tn=512):
    nl, d, n = w.shape
    rows = c_pad.shape[0]
    return pl.pallas_call(
        _ada_kernel,
        grid=(nl, n // tn),
        in_specs=[
            pl.BlockSpec((rows, d), lambda l, j: (0, 0)),
            pl.BlockSpec((1, d, tn), lambda l, j: (l, 0, j)),
            pl.BlockSpec((1, 1, tn), lambda l, j: (l, 0, j)),
        ],
        out_specs=pl.BlockSpec((1, rows, tn), lambda l, j: (l, 0, j)),
        out_shape=jax.ShapeDtypeStruct((nl, rows, n), F32),
        compiler_params=_cparams("arbitrary", "arbitrary"),
        name="ada",
    )(c_pad, w, b.reshape(nl, 1, n))


def _norm_modulate_to(h_ref, g_ref, shift_ref, scale_ref, out_ref, n_rows):
    g = g_ref[...]
    mul = 1.0 + scale_ref[0]
    add = shift_ref[0]

    def body(rows):
        x = h_ref[rows, :]
        y = (x * _inv_rms(x)) * g
        out_ref[rows, :] = (y * mul + add).astype(BF16)
    _row_loop(n_rows, body)


def _residual_gate_norm(h_ref, y_ref, gate_ref, g_ref, o_ref, n_rows):
    g = g_ref[...]
    gate = gate_ref[0]

    def body(rows):
        y = y_ref[rows, :]
        o_ref[rows, :] = h_ref[rows, :] + gate * ((y * _inv_rms(y)) * g)
    _row_loop(n_rows, body)


def _vec_spec(d, seq_tiles):
    return pl.BlockSpec((1, 1, d), lambda m, *_: (m // seq_tiles, 0, 0))


def _row_spec(d):
    return pl.BlockSpec((1, d), lambda *_: (0, 0))


def _conv_in_kernel(h_ref, shift_ref, scale_ref, g_ref, wa_ref, wg_ref, ba_ref, bg_ref, u_ref, hn_ref):
    tm = h_ref.shape[0]

    @pl.when(pl.program_id(1) == 0)
    def _():
        _norm_modulate_to(h_ref, g_ref, shift_ref, scale_ref, hn_ref, tm)

    hn = hn_ref[...]
    a = _dot(hn, wa_ref[...]) + ba_ref[...]
    g = _dot(hn, wg_ref[...]) + bg_ref[...]
    u_ref[...] = a * _sigmoid(g)


def _conv_in(h, shift, scale, g_pre, w_in, b_in, seq, tm=1024, tn=512):
    t, d = h.shape
    nt = d // tn
    return pl.pallas_call(
        _conv_in_kernel,
        grid=(t // tm, nt),
        in_specs=[
            pl.BlockSpec((tm, d), lambda m, n: (m, 0)),
            _vec_spec(d, seq // tm), _vec_spec(d, seq // tm), _row_spec(d),
            pl.BlockSpec((d, tn), lambda m, n: (0, n)),
            pl.BlockSpec((d, tn), lambda m, n: (0, n + nt)),
            pl.BlockSpec((1, tn), lambda m, n: (0, n)),
            pl.BlockSpec((1, tn), lambda m, n: (0, n + nt)),
        ],
        out_specs=pl.BlockSpec((tm, tn), lambda m, n: (m, n)),
        out_shape=jax.ShapeDtypeStruct((t, d), F32),
        scratch_shapes=[pltpu.VMEM((tm, d), BF16)],
        compiler_params=_cparams("arbitrary", "arbitrary"),
        name="conv_in",
    )(h, shift, scale, g_pre, w_in, w_in, b_in, b_in)


def _conv_out_kernel(u_ref, halo_ref, dw_ref, dwb_ref, lng_ref, lnb_ref, wout_ref, bout_ref,
                     h_ref, gate_ref, gpost_ref, o_ref, sh_ref, cv_ref, a_ref, *, seq_tiles):
    tm, d = u_ref.shape
    first = (pl.program_id(0) % seq_tiles) == 0
    sh_ref[0, 0:HALO, :] = jnp.where(first, 0.0, halo_ref[...])
    sh_ref[0, HALO:, :] = u_ref[...]

    def shift_step(i, carry):
        r0 = pl.multiple_of(i * SUBLANES, SUBLANES)
        x = sh_ref[0, pl.ds(r0, 2 * SUBLANES), :]
        for b in range(1, SUBLANES):
            sh_ref[b, pl.ds(r0, SUBLANES), :] = x[b:b + SUBLANES]
        return carry
    lax.fori_loop(0, (tm + HALO) // SUBLANES - 1, shift_step, 0)

    base = HALO - (CONV_WIDTH - 1)

    def conv_step(i, carry):
        r0 = pl.multiple_of(i * CONV_ROWS, CONV_ROWS)
        for c in range(d // CONV_LANES):
            lanes = slice(c * CONV_LANES, (c + 1) * CONV_LANES)
            acc = jnp.zeros((CONV_ROWS, CONV_LANES), F32) + dwb_ref[:, lanes]
            for j in range(CONV_WIDTH):
                a, b = divmod(base + j, SUBLANES)
                tap = sh_ref[b, pl.ds(r0 + a * SUBLANES, CONV_ROWS), lanes]
                acc = acc + tap * dw_ref[j:j + 1, lanes]
            cv_ref[pl.ds(r0, CONV_ROWS), lanes] = acc
        return carry
    lax.fori_loop(0, tm // CONV_ROWS, conv_step, 0)

    lng = lng_ref[...]
    lnb = lnb_ref[...]

    def ln_body(rows):
        x = cv_ref[rows, :]
        mu = jnp.mean(x, axis=-1, keepdims=True)
        xc = x - mu
        var = jnp.mean(xc * xc, axis=-1, keepdims=True)
        y = (xc * lax.rsqrt(var + EPS)) * lng + lnb
        a_ref[rows, :] = (y * _sigmoid(y)).astype(BF16)
    _row_loop(tm, ln_body)

    cv_ref[...] = _dot(a_ref[...], wout_ref[...]) + bout_ref[...]
    _residual_gate_norm(h_ref, cv_ref, gate_ref, gpost_ref, o_ref, tm)


def _conv_out(u, dw, dw_b, ln_g, ln_b, w_out, b_out, h, gate, g_post, seq, tm=256):
    t, d = u.shape
    seq_tiles = seq // tm
    halo_blocks = tm // HALO
    return pl.pallas_call(
        functools.partial(_conv_out_kernel, seq_tiles=seq_tiles),
        grid=(t // tm,),
        in_specs=[
            pl.BlockSpec((tm, d), lambda m: (m, 0)),
            pl.BlockSpec((HALO, d), lambda m: (jnp.maximum(m * halo_blocks - 1, 0), 0)),
            pl.BlockSpec((CONV_WIDTH, d), lambda m: (0, 0)),
            _row_spec(d), _row_spec(d), _row_spec(d),
            pl.BlockSpec((d, d), lambda m: (0, 0)),
            _row_spec(d),
            pl.BlockSpec((tm, d), lambda m: (m, 0)),
            _vec_spec(d, seq_tiles), _row_spec(d),
        ],
        out_specs=pl.BlockSpec((tm, d), lambda m: (m, 0)),
        out_shape=jax.ShapeDtypeStruct((t, d), F32),
        scratch_shapes=[pltpu.VMEM((SUBLANES, tm + HALO, d), F32), pltpu.VMEM((tm, d), F32),
                        pltpu.VMEM((tm, d), BF16)],
        compiler_params=_cparams("arbitrary"),
        name="conv_out",
    )(u, u, dw, dw_b, ln_g, ln_b, w_out, b_out, h, gate, g_post)


def _mlp_kernel(h_ref, shift_ref, scale_ref, gate_ref, gpre_ref, gpost_ref, wup_ref, wdown_ref,
                o_ref, hn_ref, acc_ref):
    tm = h_ref.shape[0]
    f = pl.program_id(1)

    @pl.when(f == 0)
    def _():
        _norm_modulate_to(h_ref, gpre_ref, shift_ref, scale_ref, hn_ref, tm)

    up = jnp.maximum(_dot(hn_ref[...], wup_ref[...]), 0.0)
    contrib = _dot((up * up).astype(BF16), wdown_ref[...])

    @pl.when(f == 0)
    def _():
        acc_ref[...] = contrib

    @pl.when(f != 0)
    def _():
        acc_ref[...] += contrib

    @pl.when(f == pl.num_programs(1) - 1)
    def _():
        _residual_gate_norm(h_ref, acc_ref, gate_ref, gpost_ref, o_ref, tm)


def _mlp(h, shift, scale, gate, g_pre, g_post, w_up, w_down, seq, tm=512, tf=512):
    t, d = h.shape
    ff = w_up.shape[1]
    seq_tiles = seq // tm
    return pl.pallas_call(
        _mlp_kernel,
        grid=(t // tm, ff // tf),
        in_specs=[
            pl.BlockSpec((tm, d), lambda m, f: (m, 0)),
            _vec_spec(d, seq_tiles), _vec_spec(d, seq_tiles), _vec_spec(d, seq_tiles),
            _row_spec(d), _row_spec(d),
            pl.BlockSpec((d, tf), lambda m, f: (0, f)),
            pl.BlockSpec((tf, d), lambda m, f: (f, 0)),
        ],
        out_specs=pl.BlockSpec((tm, d), lambda m, f: (m, 0)),
        out_shape=jax.ShapeDtypeStruct((t, d), F32),
        scratch_shapes=[pltpu.VMEM((tm, d), BF16), pltpu.VMEM((tm, d), F32)],
        compiler_params=_cparams("arbitrary", "arbitrary"),
        name="mlp",
    )(h, shift, scale, gate, g_pre, g_post, w_up, w_down)


def _rope(x, cos, sin_signed, first_half):
    swapped = jnp.where(first_half, pltpu.roll(x, LANES - QK_ROPE_DIM // 2, 1),
                        pltpu.roll(x, QK_ROPE_DIM // 2, 1))
    return x * cos + swapped * sin_signed


def _proj_kernel(h_ref, pos_ref, inv_ref, kvshift_ref, kvscale_ref, shift_ref, scale_ref,
                 gkv_ref, gpre_ref, wdkv_ref, gckv_ref, wuk_ref, wuv_ref, wdq_ref, gcq_ref, wuq_ref,
                 q_ref, k_ref, v_ref, kvn_ref, hn_ref, *, sm_scale):
    tm = h_ref.shape[0]
    gkv = gkv_ref[...]
    gpre = gpre_ref[...]
    kv_mul = 1.0 + kvscale_ref[0]
    kv_add = kvshift_ref[0]
    q_mul = 1.0 + scale_ref[0]
    q_add = shift_ref[0]

    def norm_body(rows):
        x = h_ref[rows, :]
        xn = x * _inv_rms(x)
        kvn_ref[rows, :] = ((xn * gkv) * kv_mul + kv_add).astype(BF16)
        hn_ref[rows, :] = ((xn * gpre) * q_mul + q_add).astype(BF16)
    _row_loop(tm, norm_body)

    ang = pos_ref[...].astype(F32) * inv_ref[...]
    lane = lax.broadcasted_iota(jnp.int32, (tm, LANES), 1)
    first_half = (lane % QK_ROPE_DIM) < (QK_ROPE_DIM // 2)
    cos = jnp.cos(ang)
    sin_signed = jnp.where(first_half, -jnp.sin(ang), jnp.sin(ang))

    r_kv = gckv_ref.shape[1]
    t1 = _dot(kvn_ref[...], wdkv_ref[...])
    ckv = t1[:, :r_kv]
    ckv = ((ckv * _inv_rms(ckv)) * gckv_ref[...]).astype(BF16)
    k_rope = _rope(t1[:, r_kv:], cos, sin_signed, first_half)[:, :QK_ROPE_DIM].astype(BF16)
    k_nope = _dot(ckv, wuk_ref[...]).astype(BF16)
    v = _dot(ckv, wuv_ref[...]).astype(BF16)

    cq = _dot(hn_ref[...], wdq_ref[...])
    cq = ((cq * _inv_rms(cq)) * gcq_ref[...]).astype(BF16)
    q = _dot(cq, wuq_ref[...]) * sm_scale
    n_nope = N_HEADS * QK_NOPE_DIM
    for s in range(N_HEADS * QK_ROPE_DIM // LANES):
        slab = _rope(q[:, n_nope + s * LANES:n_nope + (s + 1) * LANES], cos, sin_signed, first_half).astype(BF16)
        for half in range(LANES // QK_ROPE_DIM):
            hd = s * (LANES // QK_ROPE_DIM) + half
            q_ref[0, hd, :, QK_NOPE_DIM:] = slab[:, half * QK_ROPE_DIM:(half + 1) * QK_ROPE_DIM]
    for hd in range(N_HEADS):
        q_ref[0, hd, :, :QK_NOPE_DIM] = q[:, hd * QK_NOPE_DIM:(hd + 1) * QK_NOPE_DIM].astype(BF16)
        k_ref[0, hd, :, :QK_NOPE_DIM] = k_nope[:, hd * QK_NOPE_DIM:(hd + 1) * QK_NOPE_DIM]
        k_ref[0, hd, :, QK_NOPE_DIM:] = k_rope
        v_ref[0, hd] = v[:, hd * V_HEAD_DIM:(hd + 1) * V_HEAD_DIM]


def _proj(h, pos, inv128, kv_shift, kv_scale, shift, scale, g_kv, g_pre, w_dkvkr, g_ckv, w_uk, w_uv,
          w_dq, g_cq, w_uq, batch, seq, tm=256):
    t, d = h.shape
    seq_tiles = seq // tm
    r_kv = g_ckv.shape[1]
    r_q = g_cq.shape[1]

    def full(a):
        return pl.BlockSpec(a.shape, lambda m: (0,) * a.ndim)

    def head_spec(w):
        return pl.BlockSpec((1, N_HEADS, tm, w), lambda m: (m // seq_tiles, 0, m % seq_tiles, 0))

    return pl.pallas_call(
        functools.partial(_proj_kernel, sm_scale=QK_DIM ** -0.5),
        grid=(t // tm,),
        in_specs=[
            pl.BlockSpec((tm, d), lambda m: (m, 0)),
            pl.BlockSpec((tm, 1), lambda m: (m, 0)),
            full(inv128),
            _vec_spec(d, seq_tiles), _vec_spec(d, seq_tiles), _vec_spec(d, seq_tiles), _vec_spec(d, seq_tiles),
            _row_spec(d), _row_spec(d),
            full(w_dkvkr), _row_spec(r_kv), full(w_uk), full(w_uv), full(w_dq), _row_spec(r_q), full(w_uq),
        ],
        out_specs=[head_spec(QK_DIM), head_spec(QK_DIM), head_spec(V_HEAD_DIM)],
        out_shape=[
            jax.ShapeDtypeStruct((batch, N_HEADS, seq, QK_DIM), BF16),
            jax.ShapeDtypeStruct((batch, N_HEADS, seq, QK_DIM), BF16),
            jax.ShapeDtypeStruct((batch, N_HEADS, seq, V_HEAD_DIM), BF16),
        ],
        scratch_shapes=[pltpu.VMEM((tm, d), BF16), pltpu.VMEM((tm, d), BF16)],
        compiler_params=_cparams("arbitrary"),
        name="proj",
    )(h, pos, inv128, kv_shift, kv_scale, shift, scale, g_kv, g_pre, w_dkvkr, g_ckv, w_uk, w_uv, w_dq, g_cq, w_uq)


def _attn_kernel(q_ref, k_ref, v_ref, o_ref, *, blk):
    qi = pl.program_id(2)
    q = q_ref[0, 0]

    def scores(j):
        k = k_ref[0, 0, pl.ds(pl.multiple_of(j * blk, blk), blk), :]
        return lax.dot_general(q, k, (((1,), (1,)), ((), ())), preferred_element_type=F32)

    def update(j, s, carry):
        m, l, acc = carry
        m_new = jnp.maximum(m, jnp.max(s, axis=-1, keepdims=True))
        alpha = jnp.exp(m - m_new)
        p = jnp.exp(s - m_new)
        l = alpha * l + jnp.sum(p, axis=-1, keepdims=True)
        v = v_ref[0, 0, pl.ds(pl.multiple_of(j * blk, blk), blk), :]
        acc = alpha * acc + _dot(p.astype(BF16), v)
        return m_new, l, acc

    init = (jnp.full((blk, 1), NEG, F32), jnp.zeros((blk, 1), F32), jnp.zeros((blk, V_HEAD_DIM), F32))
    carry = lax.fori_loop(0, qi, lambda j, c: update(j, scores(j), c), init)
    row = lax.broadcasted_iota(jnp.int32, (blk, blk), 0)
    col = lax.broadcasted_iota(jnp.int32, (blk, blk), 1)
    _, l, acc = update(qi, jnp.where(col <= row, scores(qi), NEG), carry)
    o_ref[0, 0] = (acc / l).astype(BF16)


def _attention(q, k, v, blk=256):
    b, nh, s, _ = q.shape
    return pl.pallas_call(
        functools.partial(_attn_kernel, blk=blk),
        grid=(b, nh, s // blk),
        in_specs=[
            pl.BlockSpec((1, 1, blk, QK_DIM), lambda bi, hi, qi: (bi, hi, qi, 0)),
            pl.BlockSpec((1, 1, s, QK_DIM), lambda bi, hi, qi: (bi, hi, 0, 0)),
            pl.BlockSpec((1, 1, s, V_HEAD_DIM), lambda bi, hi, qi: (bi, hi, 0, 0)),
        ],
        out_specs=pl.BlockSpec((1, 1, blk, V_HEAD_DIM), lambda bi, hi, qi: (bi, hi, qi, 0)),
        out_shape=jax.ShapeDtypeStruct((b, nh, s, V_HEAD_DIM), BF16),
        compiler_params=_cparams("arbitrary", "arbitrary", "arbitrary"),
        name="attn",
    )(q, k, v)


def _attn_out_kernel(a_ref, wo_ref, h_ref, gate_ref, gpost_ref, o_ref, cat_ref, y_ref):
    tm = h_ref.shape[0]
    for hd in range(N_HEADS):
        cat_ref[:, hd * V_HEAD_DIM:(hd + 1) * V_HEAD_DIM] = a_ref[0, hd]
    y_ref[...] = _dot(cat_ref[...], wo_ref[...])
    _residual_gate_norm(h_ref, y_ref, gate_ref, gpost_ref, o_ref, tm)


def _attn_out(a, w_o, h, gate, g_post, seq, tm=256):
    t, d = h.shape
    seq_tiles = seq // tm
    return pl.pallas_call(
        _attn_out_kernel,
        grid=(t // tm,),
        in_specs=[
            pl.BlockSpec((1, N_HEADS, tm, V_HEAD_DIM), lambda m: (m // seq_tiles, 0, m % seq_tiles, 0)),
            pl.BlockSpec(w_o.shape, lambda m: (0, 0)),
            pl.BlockSpec((tm, d), lambda m: (m, 0)),
            _vec_spec(d, seq_tiles), _row_spec(d),
        ],
        out_specs=pl.BlockSpec((tm, d), lambda m: (m, 0)),
        out_shape=jax.ShapeDtypeStruct((t, d), F32),
        scratch_shapes=[pltpu.VMEM((tm, N_HEADS * V_HEAD_DIM), BF16), pltpu.VMEM((tm, d), F32)],
        compiler_params=_cparams("arbitrary"),
        name="attn_out",
    )(a, w_o, h, gate, g_post)


def kernel(x, c, positions, w_ada_mix, b_ada_mix, w_ada_mlp, b_ada_mlp, g_pre_mix, g_post_mix, g_pre_mlp,
           g_post_mlp, conv_w_in, conv_b_in, conv_dw, conv_dw_b, conv_ln_g, conv_ln_b, conv_w_out, conv_b_out,
           w_ada_kv, b_ada_kv, g_kv, w_dkv, g_ckv, w_kr, w_uk, w_uv, w_dq, g_cq, w_uq, w_o, mlp_w_up,
           mlp_w_down):
    batch, seq, d = x.shape
    depth = w_ada_mix.shape[0]
    n_conv = conv_w_in.shape[0]
    t = batch * seq

    c_pad = jnp.pad(c, ((0, SUBLANES - batch % SUBLANES), (0, 0))) if batch % SUBLANES else c

    def split(m, n):
        return [m[:batch, None, i * d:(i + 1) * d] for i in range(n)]

    ada_mix = _ada(c_pad, w_ada_mix, b_ada_mix)
    ada_mlp = _ada(c_pad, w_ada_mlp, b_ada_mlp)
    kv_shift, kv_scale = split(_ada(c_pad, w_ada_kv[None], b_ada_kv[None])[0], 2)

    def row(v):
        return v.reshape(1, -1)

    inv = 1.0 / (ROPE_THETA ** (jnp.arange(0, QK_ROPE_DIM, 2, dtype=F32) / QK_ROPE_DIM))
    inv128 = jnp.tile(inv, LANES // inv.shape[0]).reshape(1, LANES)
    pos = positions.reshape(t, 1)

    h = x.reshape(t, d)
    for l in range(depth):
        shift, scale, gate = split(ada_mix[l], 3)
        if l < n_conv:
            u = _conv_in(h, shift, scale, row(g_pre_mix[l]), conv_w_in[l].astype(BF16), row(conv_b_in[l]), seq)
            h = _conv_out(u, conv_dw[l], row(conv_dw_b[l]), row(conv_ln_g[l]), row(conv_ln_b[l]),
                          conv_w_out[l].astype(BF16), row(conv_b_out[l]), h, gate, row(g_post_mix[l]), seq)
        else:
            j = l - n_conv
            if j == 0:
                w_dkvkr = jnp.concatenate(
                    [w_dkv, w_kr, jnp.zeros((d, LANES - QK_ROPE_DIM), F32)], axis=1).astype(BF16)
                h_kv = h
            r_q = w_uq.shape[1]
            w3 = w_uq[j].reshape(r_q, N_HEADS, QK_DIM)
            w_uq_g = jnp.concatenate([w3[:, :, :QK_NOPE_DIM].reshape(r_q, -1),
                                      w3[:, :, QK_NOPE_DIM:].reshape(r_q, -1)], axis=1).astype(BF16)
            if j != 0:
                raise NotImplementedError("only one MLA layer reads the shared K/V in this trunk")
            q, k, v = _proj(h_kv, pos, inv128, kv_shift, kv_scale, shift, scale, row(g_kv), row(g_pre_mix[l]),
                            w_dkvkr, row(g_ckv), w_uk.astype(BF16), w_uv.astype(BF16), w_dq[j].astype(BF16),
                            row(g_cq[j]), w_uq_g, batch, seq)
            a = _attention(q, k, v)
            h = _attn_out(a, w_o[j].astype(BF16), h, gate, row(g_post_mix[l]), seq)
        shift, scale, gate = split(ada_mlp[l], 3)
        h = _mlp(h, shift, scale, gate, row(g_pre_mlp[l]), row(g_post_mlp[l]),
                 mlp_w_up[l].astype(BF16), mlp_w_down[l].astype(BF16), seq)
    return h.reshape(batch, seq, d)
```

```python
import functools
import math

import jax
import jax.numpy as jnp
from jax import lax
from jax.experimental import pallas as pl
from jax.experimental.pallas import tpu as pltpu

F32 = jnp.float32
BF16 = jnp.bfloat16

EPS = 1e-6
NEG = -1e30
ROPE_THETA = 10000.0

N_HEADS = 16
QK_NOPE_DIM = 128
QK_ROPE_DIM = 64
QK_DIM = QK_NOPE_DIM + QK_ROPE_DIM
V_HEAD_DIM = 128
CONV_WIDTH = 31

LANES = 128
SUBLANES = 8
VMEM_BYTES_V7X = 64 * 1024 * 1024
VMEM_LIMIT = VMEM_BYTES_V7X - 8 * 1024 * 1024

ROW_CHUNK = 16
ROW_GROUP = 4
CONV_ROWS = 32
CONV_LANES = 512
HALO = 32
PROJ_ROWS = 256
ATTN_BLOCK = 512


def _cparams(*sem):
    return pltpu.CompilerParams(dimension_semantics=sem, vmem_limit_bytes=VMEM_LIMIT)


def _dot(a, b):
    return jnp.dot(a, b, preferred_element_type=F32)


def _dot_nt(a, b):
    return lax.dot_general(a, b, (((1,), (1,)), ((), ())), preferred_element_type=F32)


def _sigmoid(x):
    return 1.0 / (1.0 + jnp.exp(-x))


def _row_loop(n_rows, body):
    span = ROW_CHUNK * ROW_GROUP

    def step(i, carry):
        r0 = pl.multiple_of(i * span, span)
        body([pl.ds(r0 + k * ROW_CHUNK, ROW_CHUNK) for k in range(ROW_GROUP)])
        return carry
    lax.fori_loop(0, n_rows // span, step, 0)


def _inv_rms(x):
    return lax.rsqrt(jnp.mean(x * x, axis=-1, keepdims=True) + EPS)


def _ada_kernel(c_ref, w_ref, b_ref, o_ref):
    c = c_ref[...]
    sc = (c * _sigmoid(c)).astype(BF16)
    o_ref[0] = _dot(sc, w_ref[0].astype(BF16)) + b_ref[0]


def _ada(c_pad, w, b, tn=512):
    nl, d, n = w.shape
    rows = c_pad.shape[0]
    return pl.pallas_call(
        _ada_kernel,
        grid=(nl, n // tn),
        in_specs=[
            pl.BlockSpec((rows, d), lambda l, j: (0, 0)),
            pl.BlockSpec((1, d, tn), lambda l, j: (l, 0, j)),
            pl.BlockSpec((1, 1, tn), lambda l, j: (l, 0, j)),
        ],
        out_specs=pl.BlockSpec((1, rows, tn), lambda l, j: (l, 0, j)),
        out_shape=jax.ShapeDtypeStruct((nl, rows, n), F32),
        compiler_params=_cparams("arbitrary", "arbitrary"),
        name="ada",
    )(c_pad, w, b.reshape(nl, 1, n))


def _norm_modulate_to(h_ref, g_ref, shift_ref, scale_ref, out_ref, n_rows):
    mul = g_ref[...] * (1.0 + scale_ref[0])
    add = shift_ref[0]

    def body(chunks):
        for rows in chunks:
            x = h_ref[rows, :]
            out_ref[rows, :] = ((x * _inv_rms(x)) * mul + add).astype(BF16)
    _row_loop(n_rows, body)


def _residual_gate_norm(h_ref, y_ref, gate_ref, g_ref, o_ref, n_rows):
    mul = gate_ref[0] * g_ref[...]

    def body(chunks):
        ys = [y_ref[rows, :] for rows in chunks]
        scaled = [y * _inv_rms(y) for y in ys]
        for rows, s in zip(chunks, scaled):
            o_ref[rows, :] = h_ref[rows, :] + s * mul
    _row_loop(n_rows, body)


def _vec_spec(d, seq_tiles):
    return pl.BlockSpec((1, 1, d), lambda m, *_: (m // seq_tiles, 0, 0))


def _row_spec(d):
    return pl.BlockSpec((1, d), lambda *_: (0, 0))


def _conv_in_kernel(h_ref, shift_ref, scale_ref, g_ref, wa_ref, wg_ref, ba_ref, bg_ref, u_ref, hn_ref):
    tm = h_ref.shape[0]

    @pl.when(pl.program_id(1) == 0)
    def _():
        _norm_modulate_to(h_ref, g_ref, shift_ref, scale_ref, hn_ref, tm)

    hn = hn_ref[...]
    a = _dot(hn, wa_ref[...].astype(BF16)) + ba_ref[...]
    g = _dot(hn, wg_ref[...].astype(BF16)) + bg_ref[...]
    u_ref[...] = a * _sigmoid(g)


def _conv_in(h, shift, scale, g_pre, w_in, b_in, layer, seq, tm=1024, tn=512):
    t, d = h.shape
    nt = d // tn
    return pl.pallas_call(
        _conv_in_kernel,
        grid=(t // tm, nt),
        in_specs=[
            pl.BlockSpec((tm, d), lambda m, n: (m, 0)),
            _vec_spec(d, seq // tm), _vec_spec(d, seq // tm), _row_spec(d),
            pl.BlockSpec((None, d, tn), lambda m, n: (layer, 0, n)),
            pl.BlockSpec((None, d, tn), lambda m, n: (layer, 0, n + nt)),
            pl.BlockSpec((1, tn), lambda m, n: (0, n)),
            pl.BlockSpec((1, tn), lambda m, n: (0, n + nt)),
        ],
        out_specs=pl.BlockSpec((tm, tn), lambda m, n: (m, n)),
        out_shape=jax.ShapeDtypeStruct((t, d), F32),
        scratch_shapes=[pltpu.VMEM((tm, d), BF16)],
        compiler_params=_cparams("arbitrary", "arbitrary"),
        name="conv_in",
    )(h, shift, scale, g_pre, w_in, w_in, b_in, b_in)


def _conv_out_kernel(u_ref, halo_ref, dw_ref, dwb_ref, lng_ref, lnb_ref, wout_ref, bout_ref,
                     h_ref, gate_ref, gpost_ref, o_ref, sh_ref, cv_ref, a_ref, *, seq_tiles):
    tm, d = u_ref.shape
    first = (pl.program_id(0) % seq_tiles) == 0
    sh_ref[0, 0:HALO, :] = jnp.where(first, 0.0, halo_ref[...])
    sh_ref[0, HALO:, :] = u_ref[...]

    def shift_step(i, carry):
        r0 = pl.multiple_of(i * SUBLANES, SUBLANES)
        x = sh_ref[0, pl.ds(r0, 2 * SUBLANES), :]
        for b in range(1, SUBLANES):
            sh_ref[b, pl.ds(r0, SUBLANES), :] = x[b:b + SUBLANES]
        return carry
    lax.fori_loop(0, (tm + HALO) // SUBLANES - 1, shift_step, 0)

    base = HALO - (CONV_WIDTH - 1)
    groups = CONV_ROWS // SUBLANES

    def conv_step(i, carry):
        r0 = pl.multiple_of(i * CONV_ROWS, CONV_ROWS)
        for c in range(d // CONV_LANES):
            lanes = slice(c * CONV_LANES, (c + 1) * CONV_LANES)
            accs = [jnp.broadcast_to(dwb_ref[:, lanes], (SUBLANES, CONV_LANES))] * groups
            for j in range(CONV_WIDTH):
                a, b = divmod(base + j, SUBLANES)
                w = dw_ref[j, :, lanes]
                for g in range(groups):
                    tap = sh_ref[b, pl.ds(r0 + (a + g) * SUBLANES, SUBLANES), lanes]
                    accs[g] = accs[g] + tap * w
            for g in range(groups):
                cv_ref[pl.ds(r0 + g * SUBLANES, SUBLANES), lanes] = accs[g]
        return carry
    lax.fori_loop(0, tm // CONV_ROWS, conv_step, 0)

    lng = lng_ref[...]
    lnb = lnb_ref[...]

    def ln_body(chunks):
        for rows in chunks:
            x = cv_ref[rows, :]
            mu = jnp.mean(x, axis=-1, keepdims=True)
            xc = x - mu
            var = jnp.mean(xc * xc, axis=-1, keepdims=True)
            y = (xc * lax.rsqrt(var + EPS)) * lng + lnb
            a_ref[rows, :] = (y * _sigmoid(y)).astype(BF16)
    _row_loop(tm, ln_body)

    cv_ref[...] = _dot(a_ref[...], wout_ref[...]) + bout_ref[...]
    _residual_gate_norm(h_ref, cv_ref, gate_ref, gpost_ref, o_ref, tm)


def _conv_out(u, dw8, dw_b, ln_g, ln_b, w_out, b_out, h, gate, g_post, seq, tm=256):
    t, d = u.shape
    seq_tiles = seq // tm
    halo_blocks = tm // HALO
    return pl.pallas_call(
        functools.partial(_conv_out_kernel, seq_tiles=seq_tiles),
        grid=(t // tm,),
        in_specs=[
            pl.BlockSpec((tm, d), lambda m: (m, 0)),
            pl.BlockSpec((HALO, d), lambda m: (jnp.maximum(m * halo_blocks - 1, 0), 0)),
            pl.BlockSpec(dw8.shape, lambda m: (0, 0, 0), pipeline_mode=pl.Buffered(1)),
            _row_spec(d), _row_spec(d), _row_spec(d),
            pl.BlockSpec((d, d), lambda m: (0, 0), pipeline_mode=pl.Buffered(1)),
            _row_spec(d),
            pl.BlockSpec((tm, d), lambda m: (m, 0)),
            _vec_spec(d, seq_tiles), _row_spec(d),
        ],
        out_specs=pl.BlockSpec((tm, d), lambda m: (m, 0)),
        out_shape=jax.ShapeDtypeStruct((t, d), F32),
        scratch_shapes=[pltpu.VMEM((SUBLANES, tm + HALO, d), F32), pltpu.VMEM((tm, d), F32),
                        pltpu.VMEM((tm, d), BF16)],
        compiler_params=_cparams("arbitrary"),
        name="conv_out",
    )(u, u, dw8, dw_b, ln_g, ln_b, w_out, b_out, h, gate, g_post)


def _mlp_kernel(h_ref, shift_ref, scale_ref, gate_ref, gpre_ref, gpost_ref, wup_ref, wdown_ref,
                o_ref, hn_ref):
    tm = h_ref.shape[0]
    f = pl.program_id(1)

    @pl.when(f == 0)
    def _():
        _norm_modulate_to(h_ref, gpre_ref, shift_ref, scale_ref, hn_ref, tm)
        o_ref[...] = jnp.zeros_like(o_ref)

    up = jnp.maximum(_dot(hn_ref[...], wup_ref[...].astype(BF16)), 0.0)
    o_ref[...] += _dot((up * up).astype(BF16), wdown_ref[...].astype(BF16))

    @pl.when(f == pl.num_programs(1) - 1)
    def _():
        _residual_gate_norm(h_ref, o_ref, gate_ref, gpost_ref, o_ref, tm)


def _mlp(h, shift, scale, gate, g_pre, g_post, w_up, w_down, layer, seq, tm=1024, tf=512):
    t, d = h.shape
    ff = w_up.shape[2]
    seq_tiles = seq // tm
    return pl.pallas_call(
        _mlp_kernel,
        grid=(t // tm, ff // tf),
        in_specs=[
            pl.BlockSpec((tm, d), lambda m, f: (m, 0), pipeline_mode=pl.Buffered(1)),
            _vec_spec(d, seq_tiles), _vec_spec(d, seq_tiles), _vec_spec(d, seq_tiles),
            _row_spec(d), _row_spec(d),
            pl.BlockSpec((None, d, tf), lambda m, f: (layer, 0, f)),
            pl.BlockSpec((None, tf, d), lambda m, f: (layer, f, 0)),
        ],
        out_specs=pl.BlockSpec((tm, d), lambda m, f: (m, 0)),
        out_shape=jax.ShapeDtypeStruct((t, d), F32),
        scratch_shapes=[pltpu.VMEM((tm, d), BF16)],
        compiler_params=_cparams("arbitrary", "arbitrary"),
        name="mlp",
    )(h, shift, scale, gate, g_pre, g_post, w_up, w_down)


def _rope(x, cos, sin_signed, first_half):
    swapped = jnp.where(first_half, pltpu.roll(x, LANES - QK_ROPE_DIM // 2, 1),
                        pltpu.roll(x, QK_ROPE_DIM // 2, 1))
    return x * cos + swapped * sin_signed


def _proj_kernel(h_ref, pos_ref, inv_ref, kvshift_ref, kvscale_ref, shift_ref, scale_ref,
                 gkv_ref, gpre_ref, wdkv_ref, gckv_ref, wuk_ref, wuvt_ref, wdq_ref, gcq_ref, wuq_ref,
                 q_ref, k_ref, vt_ref, kvn_ref, hn_ref, *, q_scale):
    tm = h_ref.shape[0]
    kv_mul = gkv_ref[...] * (1.0 + kvscale_ref[0])
    kv_add = kvshift_ref[0]
    q_mul = gpre_ref[...] * (1.0 + scale_ref[0])
    q_add = shift_ref[0]

    def norm_body(chunks):
        for rows in chunks:
            x = h_ref[rows, :]
            xn = x * _inv_rms(x)
            kvn_ref[rows, :] = (xn * kv_mul + kv_add).astype(BF16)
            hn_ref[rows, :] = (xn * q_mul + q_add).astype(BF16)
    _row_loop(tm, norm_body)

    ang = pos_ref[...].astype(F32) * inv_ref[...]
    lane = lax.broadcasted_iota(jnp.int32, (tm, LANES), 1)
    first_half = (lane % QK_ROPE_DIM) < (QK_ROPE_DIM // 2)
    cos = jnp.cos(ang)
    sin_signed = jnp.where(first_half, -jnp.sin(ang), jnp.sin(ang))

    r_kv = gckv_ref.shape[1]
    t1 = _dot(kvn_ref[...], wdkv_ref[...])
    ckv = t1[:, :r_kv]
    ckv = ((ckv * _inv_rms(ckv)) * gckv_ref[...]).astype(BF16)
    k_rope = _rope(t1[:, r_kv:], cos, sin_signed, first_half)[:, :QK_ROPE_DIM].astype(BF16)
    k_nope = _dot(ckv, wuk_ref[...]).astype(BF16)
    vt = _dot_nt(wuvt_ref[...], ckv).astype(BF16)

    cq = _dot(hn_ref[...], wdq_ref[...])
    cq = ((cq * _inv_rms(cq)) * gcq_ref[...]).astype(BF16)
    q = _dot(cq, wuq_ref[...]) * q_scale
    n_nope = N_HEADS * QK_NOPE_DIM
    for s in range(N_HEADS * QK_ROPE_DIM // LANES):
        slab = _rope(q[:, n_nope + s * LANES:n_nope + (s + 1) * LANES], cos, sin_signed, first_half).astype(BF16)
        for half in range(LANES // QK_ROPE_DIM):
            hd = s * (LANES // QK_ROPE_DIM) + half
            q_ref[0, hd, :, QK_NOPE_DIM:] = slab[:, half * QK_ROPE_DIM:(half + 1) * QK_ROPE_DIM]
    for hd in range(N_HEADS):
        q_ref[0, hd, :, :QK_NOPE_DIM] = q[:, hd * QK_NOPE_DIM:(hd + 1) * QK_NOPE_DIM].astype(BF16)
        k_ref[0, hd, :, :QK_NOPE_DIM] = k_nope[:, hd * QK_NOPE_DIM:(hd + 1) * QK_NOPE_DIM]
        k_ref[0, hd, :, QK_NOPE_DIM:] = k_rope
        vt_ref[0, hd, 0] = vt[hd * V_HEAD_DIM:(hd + 1) * V_HEAD_DIM, :]


def _proj(h, pos, inv128, kv_shift, kv_scale, shift, scale, g_kv, g_pre, w_dkvkr, g_ckv, w_uk, w_uvt,
          w_dq, g_cq, w_uq, q_scale, batch, seq):
    t, d = h.shape
    tm = PROJ_ROWS
    seq_tiles = seq // tm
    r_kv = g_ckv.shape[1]
    r_q = g_cq.shape[1]

    def full(a):
        return pl.BlockSpec(a.shape, lambda m: (0,) * a.ndim, pipeline_mode=pl.Buffered(1))

    def head_spec(w):
        return pl.BlockSpec((1, N_HEADS, tm, w), lambda m: (m // seq_tiles, 0, m % seq_tiles, 0))

    return pl.pallas_call(
        functools.partial(_proj_kernel, q_scale=q_scale),
        grid=(t // tm,),
        in_specs=[
            pl.BlockSpec((tm, d), lambda m: (m, 0)),
            pl.BlockSpec((tm, 1), lambda m: (m, 0)),
            full(inv128),
            _vec_spec(d, seq_tiles), _vec_spec(d, seq_tiles), _vec_spec(d, seq_tiles), _vec_spec(d, seq_tiles),
            _row_spec(d), _row_spec(d),
            full(w_dkvkr), _row_spec(r_kv), full(w_uk), full(w_uvt), full(w_dq), _row_spec(r_q), full(w_uq),
        ],
        out_specs=[
            head_spec(QK_DIM), head_spec(QK_DIM),
            pl.BlockSpec((1, N_HEADS, 1, V_HEAD_DIM, tm), lambda m: (m // seq_tiles, 0, m % seq_tiles, 0, 0)),
        ],
        out_shape=[
            jax.ShapeDtypeStruct((batch, N_HEADS, seq, QK_DIM), BF16),
            jax.ShapeDtypeStruct((batch, N_HEADS, seq, QK_DIM), BF16),
            jax.ShapeDtypeStruct((batch, N_HEADS, seq_tiles, V_HEAD_DIM, tm), BF16),
        ],
        scratch_shapes=[pltpu.VMEM((tm, d), BF16), pltpu.VMEM((tm, d), BF16)],
        compiler_params=_cparams("arbitrary"),
        name="proj",
    )(h, pos, inv128, kv_shift, kv_scale, shift, scale, g_kv, g_pre, w_dkvkr, g_ckv, w_uk, w_uvt, w_dq, g_cq, w_uq)


def _attn_kernel(q_ref, k_ref, vt_ref, o_ref, sa_ref, sb_ref, acc_ref, *, blk):
    qi = pl.program_id(2)
    q = q_ref[0, 0]
    half = vt_ref.shape[-1]
    assert blk == 2 * half

    def scores_to(s_ref, tile):
        k = k_ref[0, 0, pl.ds(pl.multiple_of(tile * half, half), half), :]
        s = _dot_nt(k, q)
        s_ref[...] = s
        return jnp.max(s, axis=0, keepdims=True)

    def update(s, mx, tile, m, l):
        m_new = jnp.maximum(m, mx)
        alpha = jnp.exp2(m - m_new)
        p = jnp.exp2(s - m_new)
        l = alpha * l + jnp.sum(p, axis=0, keepdims=True)
        acc_ref[...] = alpha * acc_ref[...] + _dot(vt_ref[0, 0, tile], p.astype(BF16))
        return m_new, l

    def body(j, carry):
        mxa, m, l = carry
        mxb = scores_to(sb_ref, 2 * j + 1)
        m, l = update(sa_ref[...], mxa, 2 * j, m, l)
        mxa = scores_to(sa_ref, 2 * j + 2)
        m, l = update(sb_ref[...], mxb, 2 * j + 1, m, l)
        return mxa, m, l

    acc_ref[...] = jnp.zeros_like(acc_ref)
    init = (scores_to(sa_ref, 0), jnp.full((1, blk), NEG, F32), jnp.zeros((1, blk), F32))
    _, m, l = lax.fori_loop(0, qi, body, init)

    scores_to(sb_ref, 2 * qi + 1)
    key = lax.broadcasted_iota(jnp.int32, (half, blk), 0)
    qry = lax.broadcasted_iota(jnp.int32, (half, blk), 1)
    sa = jnp.where(key <= qry, sa_ref[...], NEG)
    m, l = update(sa, jnp.max(sa, axis=0, keepdims=True), 2 * qi, m, l)
    sb = jnp.where(key + half <= qry, sb_ref[...], NEG)
    m, l = update(sb, jnp.max(sb, axis=0, keepdims=True), 2 * qi + 1, m, l)
    o_ref[0, 0] = (acc_ref[...] / l).T.astype(BF16)


def _attention(q, k, vt, blk=ATTN_BLOCK):
    b, nh, s, _ = q.shape
    half = vt.shape[-1]
    return pl.pallas_call(
        functools.partial(_attn_kernel, blk=blk),
        grid=(b, nh, s // blk),
        in_specs=[
            pl.BlockSpec((1, 1, blk, QK_DIM), lambda bi, hi, qi: (bi, hi, qi, 0)),
            pl.BlockSpec((1, 1, s, QK_DIM), lambda bi, hi, qi: (bi, hi, 0, 0)),
            pl.BlockSpec((1, 1) + vt.shape[2:], lambda bi, hi, qi: (bi, hi, 0, 0, 0)),
        ],
        out_specs=pl.BlockSpec((1, 1, blk, V_HEAD_DIM), lambda bi, hi, qi: (bi, hi, qi, 0)),
        out_shape=jax.ShapeDtypeStruct((b, nh, s, V_HEAD_DIM), BF16),
        scratch_shapes=[pltpu.VMEM((half, blk), F32), pltpu.VMEM((half, blk), F32),
                        pltpu.VMEM((V_HEAD_DIM, blk), F32)],
        compiler_params=_cparams("arbitrary", "arbitrary", "arbitrary"),
        name="attn",
    )(q, k, vt)


def _attn_out_kernel(a_ref, wo_ref, h_ref, gate_ref, gpost_ref, o_ref, cat_ref):
    tm = h_ref.shape[0]
    for hd in range(N_HEADS):
        cat_ref[:, hd * V_HEAD_DIM:(hd + 1) * V_HEAD_DIM] = a_ref[0, hd]
    o_ref[...] = _dot(cat_ref[...], wo_ref[...])
    _residual_gate_norm(h_ref, o_ref, gate_ref, gpost_ref, o_ref, tm)


def _attn_out(a, w_o, h, gate, g_post, seq, tm=512):
    t, d = h.shape
    seq_tiles = seq // tm
    return pl.pallas_call(
        _attn_out_kernel,
        grid=(t // tm,),
        in_specs=[
            pl.BlockSpec((1, N_HEADS, tm, V_HEAD_DIM), lambda m: (m // seq_tiles, 0, m % seq_tiles, 0)),
            pl.BlockSpec(w_o.shape, lambda m: (0, 0), pipeline_mode=pl.Buffered(1)),
            pl.BlockSpec((tm, d), lambda m: (m, 0)),
            _vec_spec(d, seq_tiles), _row_spec(d),
        ],
        out_specs=pl.BlockSpec((tm, d), lambda m: (m, 0)),
        out_shape=jax.ShapeDtypeStruct((t, d), F32),
        scratch_shapes=[pltpu.VMEM((tm, N_HEADS * V_HEAD_DIM), BF16)],
        compiler_params=_cparams("arbitrary"),
        name="attn_out",
    )(a, w_o, h, gate, g_post)


def kernel(x, c, positions, w_ada_mix, b_ada_mix, w_ada_mlp, b_ada_mlp, g_pre_mix, g_post_mix, g_pre_mlp,
           g_post_mlp, conv_w_in, conv_b_in, conv_dw, conv_dw_b, conv_ln_g, conv_ln_b, conv_w_out, conv_b_out,
           w_ada_kv, b_ada_kv, g_kv, w_dkv, g_ckv, w_kr, w_uk, w_uv, w_dq, g_cq, w_uq, w_o, mlp_w_up,
           mlp_w_down):
    batch, seq, d = x.shape
    depth = w_ada_mix.shape[0]
    n_conv = conv_w_in.shape[0]
    t = batch * seq
    if depth - n_conv != 1:
        raise NotImplementedError("exactly one MLA layer reads the shared K/V in this trunk")

    c_pad = jnp.pad(c, ((0, -batch % SUBLANES), (0, 0)))

    def split(m, n):
        return [m[:batch, None, i * d:(i + 1) * d] for i in range(n)]

    ada_mix = _ada(c_pad, w_ada_mix, b_ada_mix)
    ada_mlp = _ada(c_pad, w_ada_mlp, b_ada_mlp)
    kv_shift, kv_scale = split(_ada(c_pad, w_ada_kv[None], b_ada_kv[None])[0], 2)

    def row(v):
        return v.reshape(1, -1)

    inv = 1.0 / (ROPE_THETA ** (jnp.arange(0, QK_ROPE_DIM, 2, dtype=F32) / QK_ROPE_DIM))
    inv128 = jnp.tile(inv, LANES // inv.shape[0]).reshape(1, LANES)
    pos = positions.reshape(t, 1)
    q_scale = QK_DIM ** -0.5 * math.log2(math.e)

    h = x.reshape(t, d)
    for l in range(depth):
        shift, scale, gate = split(ada_mix[l], 3)
        if l < n_conv:
            u = _conv_in(h, shift, scale, row(g_pre_mix[l]), conv_w_in, row(conv_b_in[l]), l, seq)
            dw8 = jnp.broadcast_to(conv_dw[l][:, None, :], (CONV_WIDTH, SUBLANES, d))
            h = _conv_out(u, dw8, row(conv_dw_b[l]), row(conv_ln_g[l]), row(conv_ln_b[l]),
                          conv_w_out[l].astype(BF16), row(conv_b_out[l]), h, gate, row(g_post_mix[l]), seq)
        else:
            j = l - n_conv
            w_dkvkr = jnp.concatenate(
                [w_dkv, w_kr, jnp.zeros((d, LANES - QK_ROPE_DIM), F32)], axis=1).astype(BF16)
            r_q = w_uq.shape[1]
            w3 = w_uq[j].reshape(r_q, N_HEADS, QK_DIM)
            w_uq_g = jnp.concatenate([w3[:, :, :QK_NOPE_DIM].reshape(r_q, -1),
                                      w3[:, :, QK_NOPE_DIM:].reshape(r_q, -1)], axis=1).astype(BF16)
            q, k, vt = _proj(h, pos, inv128, kv_shift, kv_scale, shift, scale, row(g_kv), row(g_pre_mix[l]),
                             w_dkvkr, row(g_ckv), w_uk.astype(BF16), w_uv.T.astype(BF16), w_dq[j].astype(BF16),
                             row(g_cq[j]), w_uq_g, q_scale, batch, seq)
            a = _attention(q, k, vt)
            h = _attn_out(a, w_o[j].astype(BF16), h, gate, row(g_post_mix[l]), seq)
        shift, scale, gate = split(ada_mlp[l], 3)
        h = _mlp(h, shift, scale, gate, row(g_pre_mlp[l]), row(g_post_mlp[l]), mlp_w_up, mlp_w_down, l, seq)
    return h.reshape(batch, seq, d)
```

```python
import functools
import math

import jax
import jax.numpy as jnp
from jax import lax
from jax.experimental import pallas as pl
from jax.experimental.pallas import tpu as pltpu

F32 = jnp.float32
BF16 = jnp.bfloat16

EPS = 1e-6
NEG = -1e30
ROPE_THETA = 10000.0

N_HEADS = 16
QK_NOPE_DIM = 128
QK_ROPE_DIM = 64
QK_DIM = QK_NOPE_DIM + QK_ROPE_DIM
V_HEAD_DIM = 128
CONV_WIDTH = 31

LANES = 128
SUBLANES = 8
VMEM_BYTES_V7X = 64 * 1024 * 1024
VMEM_LIMIT = VMEM_BYTES_V7X - 8 * 1024 * 1024

ROW_CHUNK = 16
ROW_GROUP = 4
CONV_ROWS = 32
CONV_LANES = 256
HALO = 32
PROJ_ROWS = 256
ATTN_BLOCK = 512
ATTN_HEADS = 4


def _cparams(*sem, flags=None):
    return pltpu.CompilerParams(dimension_semantics=sem, vmem_limit_bytes=VMEM_LIMIT, flags=flags)


def _dot(a, b):
    return jnp.dot(a, b, preferred_element_type=F32)


def _dot_nt(a, b):
    return lax.dot_general(a, b, (((1,), (1,)), ((), ())), preferred_element_type=F32)


def _sigmoid(x):
    return 1.0 / (1.0 + jnp.exp(-x))


def _row_loop(n_rows, body):
    span = ROW_CHUNK * ROW_GROUP

    def step(i, carry):
        r0 = pl.multiple_of(i * span, span)
        body([pl.ds(r0 + k * ROW_CHUNK, ROW_CHUNK) for k in range(ROW_GROUP)])
        return carry
    lax.fori_loop(0, n_rows // span, step, 0)


def _inv_rms(x):
    return lax.rsqrt(jnp.mean(x * x, axis=-1, keepdims=True) + EPS)


def _ada_kernel(c_ref, w_ref, b_ref, o_ref):
    c = c_ref[...]
    sc = (c * _sigmoid(c)).astype(BF16)
    o_ref[0] = _dot(sc, w_ref[0].astype(BF16)) + b_ref[0]


def _ada(c_pad, w, b, tn=512):
    nl, d, n = w.shape
    rows = c_pad.shape[0]
    return pl.pallas_call(
        _ada_kernel,
        grid=(nl, n // tn),
        in_specs=[
            pl.BlockSpec((rows, d), lambda l, j: (0, 0)),
            pl.BlockSpec((1, d, tn), lambda l, j: (l, 0, j)),
            pl.BlockSpec((1, 1, tn), lambda l, j: (l, 0, j)),
        ],
        out_specs=pl.BlockSpec((1, rows, tn), lambda l, j: (l, 0, j)),
        out_shape=jax.ShapeDtypeStruct((nl, rows, n), F32),
        compiler_params=_cparams("arbitrary", "arbitrary"),
        name="ada",
    )(c_pad, w, b.reshape(nl, 1, n))


def _norm_modulate_to(h_ref, g_ref, shift_ref, scale_ref, out_ref, n_rows):
    mul = g_ref[...] * (1.0 + scale_ref[0])
    add = shift_ref[0]

    def body(chunks):
        for rows in chunks:
            x = h_ref[rows, :]
            out_ref[rows, :] = ((x * _inv_rms(x)) * mul + add).astype(BF16)
    _row_loop(n_rows, body)


def _residual_gate_norm(h_ref, y_ref, gate_ref, g_ref, o_ref, n_rows):
    mul = gate_ref[0] * g_ref[...]

    def body(chunks):
        ys = [y_ref[rows, :] for rows in chunks]
        scaled = [y * _inv_rms(y) for y in ys]
        for rows, s in zip(chunks, scaled):
            o_ref[rows, :] = h_ref[rows, :] + s * mul
    _row_loop(n_rows, body)


def _vec_spec(d, seq_tiles):
    return pl.BlockSpec((1, 1, d), lambda m, *_: (m // seq_tiles, 0, 0))


def _row_spec(d):
    return pl.BlockSpec((1, d), lambda *_: (0, 0))


def _conv_in_kernel(h_ref, shift_ref, scale_ref, g_ref, wa_ref, wg_ref, ba_ref, bg_ref, u_ref, hn_ref):
    tm = h_ref.shape[0]

    @pl.when(pl.program_id(1) == 0)
    def _():
        _norm_modulate_to(h_ref, g_ref, shift_ref, scale_ref, hn_ref, tm)

    hn = hn_ref[...]
    a = _dot(hn, wa_ref[...].astype(BF16)) + ba_ref[...]
    g = _dot(hn, wg_ref[...].astype(BF16)) + bg_ref[...]
    u_ref[...] = a * _sigmoid(g)


def _conv_in(h, shift, scale, g_pre, w_in, b_in, layer, seq, tm=1024, tn=512):
    t, d = h.shape
    nt = d // tn
    return pl.pallas_call(
        _conv_in_kernel,
        grid=(t // tm, nt),
        in_specs=[
            pl.BlockSpec((tm, d), lambda m, n: (m, 0)),
            _vec_spec(d, seq // tm), _vec_spec(d, seq // tm), _row_spec(d),
            pl.BlockSpec((None, d, tn), lambda m, n: (layer, 0, n)),
            pl.BlockSpec((None, d, tn), lambda m, n: (layer, 0, n + nt)),
            pl.BlockSpec((1, tn), lambda m, n: (0, n)),
            pl.BlockSpec((1, tn), lambda m, n: (0, n + nt)),
        ],
        out_specs=pl.BlockSpec((tm, tn), lambda m, n: (m, n)),
        out_shape=jax.ShapeDtypeStruct((t, d), F32),
        scratch_shapes=[pltpu.VMEM((tm, d), BF16)],
        compiler_params=_cparams("arbitrary", "arbitrary"),
        name="conv_in",
    )(h, shift, scale, g_pre, w_in, w_in, b_in, b_in)


def _conv_out_kernel(u_ref, halo_ref, dw_ref, dwb_ref, lng_ref, lnb_ref, wout_ref, bout_ref,
                     h_ref, gate_ref, gpost_ref, o_ref, sh_ref, cv_ref, a_ref, *, seq_tiles):
    tm, d = u_ref.shape
    first = (pl.program_id(0) % seq_tiles) == 0
    sh_ref[0, 0:HALO, :] = jnp.where(first, 0.0, halo_ref[...])
    sh_ref[0, HALO:, :] = u_ref[...]

    def shift_step(i, carry):
        r0 = pl.multiple_of(i * SUBLANES, SUBLANES)
        x = sh_ref[0, pl.ds(r0, 2 * SUBLANES), :]
        for b in range(1, SUBLANES):
            sh_ref[b, pl.ds(r0, SUBLANES), :] = x[b:b + SUBLANES]
        return carry
    lax.fori_loop(0, (tm + HALO) // SUBLANES - 1, shift_step, 0)

    base = HALO - (CONV_WIDTH - 1)
    groups = CONV_ROWS // SUBLANES

    def conv_step(i, carry):
        r0 = pl.multiple_of(i * CONV_ROWS, CONV_ROWS)
        for c in range(d // CONV_LANES):
            lanes = slice(c * CONV_LANES, (c + 1) * CONV_LANES)
            accs = [jnp.broadcast_to(dwb_ref[:, lanes], (SUBLANES, CONV_LANES))] * groups
            for j in range(CONV_WIDTH):
                a, b = divmod(base + j, SUBLANES)
                w = dw_ref[j, :, lanes]
                for g in range(groups):
                    tap = sh_ref[b, pl.ds(r0 + (a + g) * SUBLANES, SUBLANES), lanes]
                    accs[g] = accs[g] + tap * w
            for g in range(groups):
                cv_ref[pl.ds(r0 + g * SUBLANES, SUBLANES), lanes] = accs[g]
        return carry
    lax.fori_loop(0, tm // CONV_ROWS, conv_step, 0)

    lng = lng_ref[...]
    lnb = lnb_ref[...]

    def ln_body(chunks):
        for rows in chunks:
            x = cv_ref[rows, :]
            mu = jnp.mean(x, axis=-1, keepdims=True)
            xc = x - mu
            var = jnp.mean(xc * xc, axis=-1, keepdims=True)
            y = (xc * lax.rsqrt(var + EPS)) * lng + lnb
            a_ref[rows, :] = (y * _sigmoid(y)).astype(BF16)
    _row_loop(tm, ln_body)

    cv_ref[...] = _dot(a_ref[...], wout_ref[...]) + bout_ref[...]
    _residual_gate_norm(h_ref, cv_ref, gate_ref, gpost_ref, o_ref, tm)


def _conv_out(u, dw8, dw_b, ln_g, ln_b, w_out, b_out, h, gate, g_post, seq, tm=256):
    t, d = u.shape
    seq_tiles = seq // tm
    halo_blocks = tm // HALO
    return pl.pallas_call(
        functools.partial(_conv_out_kernel, seq_tiles=seq_tiles),
        grid=(t // tm,),
        in_specs=[
            pl.BlockSpec((tm, d), lambda m: (m, 0)),
            pl.BlockSpec((HALO, d), lambda m: (jnp.maximum(m * halo_blocks - 1, 0), 0)),
            pl.BlockSpec(dw8.shape, lambda m: (0, 0, 0), pipeline_mode=pl.Buffered(1)),
            _row_spec(d), _row_spec(d), _row_spec(d),
            pl.BlockSpec((d, d), lambda m: (0, 0), pipeline_mode=pl.Buffered(1)),
            _row_spec(d),
            pl.BlockSpec((tm, d), lambda m: (m, 0)),
            _vec_spec(d, seq_tiles), _row_spec(d),
        ],
        out_specs=pl.BlockSpec((tm, d), lambda m: (m, 0)),
        out_shape=jax.ShapeDtypeStruct((t, d), F32),
        scratch_shapes=[pltpu.VMEM((SUBLANES, tm + HALO, d), F32), pltpu.VMEM((tm, d), F32),
                        pltpu.VMEM((tm, d), BF16)],
        compiler_params=_cparams("arbitrary"),
        name="conv_out",
    )(u, u, dw8, dw_b, ln_g, ln_b, w_out, b_out, h, gate, g_post)


def _mlp_kernel(h_ref, shift_ref, scale_ref, gate_ref, gpre_ref, gpost_ref, wup_ref, wdown_ref,
                o_ref, hn_ref):
    tm = h_ref.shape[0]
    f = pl.program_id(1)

    @pl.when(f == 0)
    def _():
        _norm_modulate_to(h_ref, gpre_ref, shift_ref, scale_ref, hn_ref, tm)
        o_ref[...] = jnp.zeros_like(o_ref)

    up = jnp.maximum(_dot(hn_ref[...], wup_ref[...].astype(BF16)), 0.0)
    o_ref[...] += _dot((up * up).astype(BF16), wdown_ref[...].astype(BF16))

    @pl.when(f == pl.num_programs(1) - 1)
    def _():
        _residual_gate_norm(h_ref, o_ref, gate_ref, gpost_ref, o_ref, tm)


def _mlp(h, shift, scale, gate, g_pre, g_post, w_up, w_down, layer, seq, tm=1024, tf=512):
    t, d = h.shape
    ff = w_up.shape[2]
    seq_tiles = seq // tm
    h_buffers = 1 if w_up.dtype == F32 else 2
    return pl.pallas_call(
        _mlp_kernel,
        grid=(t // tm, ff // tf),
        in_specs=[
            pl.BlockSpec((tm, d), lambda m, f: (m, 0), pipeline_mode=pl.Buffered(h_buffers)),
            _vec_spec(d, seq_tiles), _vec_spec(d, seq_tiles), _vec_spec(d, seq_tiles),
            _row_spec(d), _row_spec(d),
            pl.BlockSpec((None, d, tf), lambda m, f: (layer, 0, f)),
            pl.BlockSpec((None, tf, d), lambda m, f: (layer, f, 0)),
        ],
        out_specs=pl.BlockSpec((tm, d), lambda m, f: (m, 0)),
        out_shape=jax.ShapeDtypeStruct((t, d), F32),
        scratch_shapes=[pltpu.VMEM((tm, d), BF16)],
        compiler_params=_cparams("arbitrary", "arbitrary"),
        name="mlp",
    )(h, shift, scale, gate, g_pre, g_post, w_up, w_down)


def _rope_t(x, cos, sin):
    half = QK_ROPE_DIM // 2
    x1, x2 = x[:half], x[half:]
    return x1 * cos - x2 * sin, x2 * cos + x1 * sin


def _proj_kernel(h_ref, pos_ref, inv_ref, kvshift_ref, kvscale_ref, shift_ref, scale_ref,
                 gkv_ref, gpre_ref, wdkv_ref, gckv_ref, wuk_ref, wuvt_ref, wdq_ref, gcq_ref, wuqt_ref,
                 qt_ref, k_ref, vt_ref, kvn_ref, hn_ref, *, q_scale):
    tm = h_ref.shape[0]
    kv_mul = gkv_ref[...] * (1.0 + kvscale_ref[0])
    kv_add = kvshift_ref[0]
    q_mul = gpre_ref[...] * (1.0 + scale_ref[0])
    q_add = shift_ref[0]

    def norm_body(chunks):
        for rows in chunks:
            x = h_ref[rows, :]
            xn = x * _inv_rms(x)
            kvn_ref[rows, :] = (xn * kv_mul + kv_add).astype(BF16)
            hn_ref[rows, :] = (xn * q_mul + q_add).astype(BF16)
    _row_loop(tm, norm_body)

    ang = inv_ref[...] * pos_ref[...].astype(F32)
    cos = jnp.cos(ang)
    sin = jnp.sin(ang)

    r_kv = gckv_ref.shape[1]
    t1 = _dot(kvn_ref[...], wdkv_ref[...])
    ckv = t1[:, :r_kv]
    ckv = ((ckv * _inv_rms(ckv)) * gckv_ref[...]).astype(BF16)
    kr1, kr2 = _rope_t(t1[:, r_kv:].T[:QK_ROPE_DIM], cos, sin)
    k_rope = jnp.concatenate([kr1, kr2, jnp.zeros((LANES - QK_ROPE_DIM, tm), F32)], axis=0).T
    k_rope = k_rope[:, :QK_ROPE_DIM].astype(BF16)
    k_nope = _dot(ckv, wuk_ref[...]).astype(BF16)
    vt = _dot_nt(wuvt_ref[...], ckv).astype(BF16)

    cq = _dot(hn_ref[...], wdq_ref[...])
    cq = ((cq * _inv_rms(cq)) * gcq_ref[...]).astype(BF16)
    qt = _dot_nt(wuqt_ref[...], cq) * q_scale
    for hd in range(N_HEADS):
        r0 = hd * QK_DIM
        q1, q2 = _rope_t(qt[r0 + QK_NOPE_DIM:r0 + QK_DIM], cos, sin)
        qt_ref[0, hd, :QK_NOPE_DIM, :] = qt[r0:r0 + QK_NOPE_DIM].astype(BF16)
        qt_ref[0, hd, QK_NOPE_DIM:QK_NOPE_DIM + QK_ROPE_DIM // 2, :] = q1.astype(BF16)
        qt_ref[0, hd, QK_NOPE_DIM + QK_ROPE_DIM // 2:, :] = q2.astype(BF16)
        k_ref[0, hd, :, :QK_NOPE_DIM] = k_nope[:, hd * QK_NOPE_DIM:(hd + 1) * QK_NOPE_DIM]
        k_ref[0, hd, :, QK_NOPE_DIM:] = k_rope
        vt_ref[0, hd, 0] = vt[hd * V_HEAD_DIM:(hd + 1) * V_HEAD_DIM, :]


def _proj(h, pos, inv_col, kv_shift, kv_scale, shift, scale, g_kv, g_pre, w_dkvkr, g_ckv, w_uk, w_uvt,
          w_dq, g_cq, w_uqt, q_scale, batch, seq):
    t, d = h.shape
    tm = PROJ_ROWS
    seq_tiles = seq // tm
    r_kv = g_ckv.shape[1]
    r_q = g_cq.shape[1]

    def full(a):
        return pl.BlockSpec(a.shape, lambda m: (0,) * a.ndim, pipeline_mode=pl.Buffered(1))

    return pl.pallas_call(
        functools.partial(_proj_kernel, q_scale=q_scale),
        grid=(t // tm,),
        in_specs=[
            pl.BlockSpec((tm, d), lambda m: (m, 0)),
            pl.BlockSpec((1, tm), lambda m: (0, m)),
            full(inv_col),
            _vec_spec(d, seq_tiles), _vec_spec(d, seq_tiles), _vec_spec(d, seq_tiles), _vec_spec(d, seq_tiles),
            _row_spec(d), _row_spec(d),
            full(w_dkvkr), _row_spec(r_kv), full(w_uk), full(w_uvt), full(w_dq), _row_spec(r_q), full(w_uqt),
        ],
        out_specs=[
            pl.BlockSpec((1, N_HEADS, QK_DIM, tm), lambda m: (m // seq_tiles, 0, 0, m % seq_tiles)),
            pl.BlockSpec((1, N_HEADS, tm, QK_DIM), lambda m: (m // seq_tiles, 0, m % seq_tiles, 0)),
            pl.BlockSpec((1, N_HEADS, 1, V_HEAD_DIM, tm), lambda m: (m // seq_tiles, 0, m % seq_tiles, 0, 0)),
        ],
        out_shape=[
            jax.ShapeDtypeStruct((batch, N_HEADS, QK_DIM, seq), BF16),
            jax.ShapeDtypeStruct((batch, N_HEADS, seq, QK_DIM), BF16),
            jax.ShapeDtypeStruct((batch, N_HEADS, seq_tiles, V_HEAD_DIM, tm), BF16),
        ],
        scratch_shapes=[pltpu.VMEM((tm, d), BF16), pltpu.VMEM((tm, d), BF16)],
        compiler_params=_cparams("arbitrary"),
        name="proj",
    )(h, pos, inv_col, kv_shift, kv_scale, shift, scale, g_kv, g_pre, w_dkvkr, g_ckv, w_uk, w_uvt, w_dq, g_cq, w_uqt)


def _attn_kernel(qt_ref, k_ref, vt_ref, o_ref, sa_ref, sb_ref, acc_ref, *, blk):
    qi = pl.program_id(2)
    heads = range(qt_ref.shape[1])
    half = vt_ref.shape[-1]
    assert blk == 2 * half

    def scores_to(s_ref, hd, tile):
        k = k_ref[0, hd, pl.ds(pl.multiple_of(tile * half, half), half), :]
        s = _dot(k, qt_ref[0, hd])
        s_ref[hd] = s
        return jnp.max(s, axis=0, keepdims=True)

    def update(s, mx, hd, tile, m, l):
        m_new = jnp.maximum(m, mx)
        alpha = jnp.exp2(m - m_new)
        p = jnp.exp2(s - m_new)
        l = alpha * l + jnp.sum(p, axis=0, keepdims=True)
        acc_ref[hd] = alpha * acc_ref[hd] + _dot(vt_ref[0, hd, tile], p.astype(BF16))
        return m_new, l

    def body(j, carry):
        out = []
        for hd, (mxa, m, l) in zip(heads, carry):
            mxb = scores_to(sb_ref, hd, 2 * j + 1)
            m, l = update(sa_ref[hd], mxa, hd, 2 * j, m, l)
            mxa = scores_to(sa_ref, hd, 2 * j + 2)
            m, l = update(sb_ref[hd], mxb, hd, 2 * j + 1, m, l)
            out.append((mxa, m, l))
        return tuple(out)

    acc_ref[...] = jnp.zeros_like(acc_ref)
    init = tuple((scores_to(sa_ref, hd, 0), jnp.full((1, blk), NEG, F32), jnp.zeros((1, blk), F32))
                 for hd in heads)
    carry = lax.fori_loop(0, qi, body, init)

    key = lax.broadcasted_iota(jnp.int32, (half, blk), 0)
    qry = lax.broadcasted_iota(jnp.int32, (half, blk), 1)
    for hd, (_, m, l) in zip(heads, carry):
        scores_to(sb_ref, hd, 2 * qi + 1)
        sa = jnp.where(key <= qry, sa_ref[hd], NEG)
        m, l = update(sa, jnp.max(sa, axis=0, keepdims=True), hd, 2 * qi, m, l)
        sb = jnp.where(key + half <= qry, sb_ref[hd], NEG)
        m, l = update(sb, jnp.max(sb, axis=0, keepdims=True), hd, 2 * qi + 1, m, l)
        o_ref[0, hd] = (acc_ref[hd] / l).T.astype(BF16)


def _attention(qt, k, vt, blk=ATTN_BLOCK, nh=ATTN_HEADS):
    b, n_heads, s, _ = k.shape
    half = vt.shape[-1]
    return pl.pallas_call(
        functools.partial(_attn_kernel, blk=blk),
        grid=(b, n_heads // nh, s // blk),
        in_specs=[
            pl.BlockSpec((1, nh, QK_DIM, blk), lambda bi, hi, qi: (bi, hi, 0, qi)),
            pl.BlockSpec((1, nh, s, QK_DIM), lambda bi, hi, qi: (bi, hi, 0, 0)),
            pl.BlockSpec((1, nh) + vt.shape[2:], lambda bi, hi, qi: (bi, hi, 0, 0, 0)),
        ],
        out_specs=pl.BlockSpec((1, nh, blk, V_HEAD_DIM), lambda bi, hi, qi: (bi, hi, qi, 0)),
        out_shape=jax.ShapeDtypeStruct((b, n_heads, s, V_HEAD_DIM), BF16),
        scratch_shapes=[pltpu.VMEM((nh, half, blk), F32), pltpu.VMEM((nh, half, blk), F32),
                        pltpu.VMEM((nh, V_HEAD_DIM, blk), F32)],
        compiler_params=_cparams("arbitrary", "arbitrary", "arbitrary"),
        name="attn",
    )(qt, k, vt)


def _attn_out_kernel(a_ref, wo_ref, h_ref, gate_ref, gpost_ref, o_ref, cat_ref):
    tm = h_ref.shape[0]
    for hd in range(N_HEADS):
        cat_ref[:, hd * V_HEAD_DIM:(hd + 1) * V_HEAD_DIM] = a_ref[0, hd]
    o_ref[...] = _dot(cat_ref[...], wo_ref[...])
    _residual_gate_norm(h_ref, o_ref, gate_ref, gpost_ref, o_ref, tm)


def _attn_out(a, w_o, h, gate, g_post, seq, tm=512):
    t, d = h.shape
    seq_tiles = seq // tm
    return pl.pallas_call(
        _attn_out_kernel,
        grid=(t // tm,),
        in_specs=[
            pl.BlockSpec((1, N_HEADS, tm, V_HEAD_DIM), lambda m: (m // seq_tiles, 0, m % seq_tiles, 0)),
            pl.BlockSpec(w_o.shape, lambda m: (0, 0), pipeline_mode=pl.Buffered(1)),
            pl.BlockSpec((tm, d), lambda m: (m, 0)),
            _vec_spec(d, seq_tiles), _row_spec(d),
        ],
        out_specs=pl.BlockSpec((tm, d), lambda m: (m, 0)),
        out_shape=jax.ShapeDtypeStruct((t, d), F32),
        scratch_shapes=[pltpu.VMEM((tm, N_HEADS * V_HEAD_DIM), BF16)],
        compiler_params=_cparams("arbitrary"),
        name="attn_out",
    )(a, w_o, h, gate, g_post)


def kernel(x, c, positions, w_ada_mix, b_ada_mix, w_ada_mlp, b_ada_mlp, g_pre_mix, g_post_mix, g_pre_mlp,
           g_post_mlp, conv_w_in, conv_b_in, conv_dw, conv_dw_b, conv_ln_g, conv_ln_b, conv_w_out, conv_b_out,
           w_ada_kv, b_ada_kv, g_kv, w_dkv, g_ckv, w_kr, w_uk, w_uv, w_dq, g_cq, w_uq, w_o, mlp_w_up,
           mlp_w_down):
    batch, seq, d = x.shape
    depth = w_ada_mix.shape[0]
    n_conv = conv_w_in.shape[0]
    t = batch * seq
    if depth - n_conv != 1:
        raise NotImplementedError("exactly one MLA layer reads the shared K/V in this trunk")

    c_pad = jnp.pad(c, ((0, -batch % SUBLANES), (0, 0)))

    def split(m, n):
        return [m[:batch, None, i * d:(i + 1) * d] for i in range(n)]

    ada_mix = _ada(c_pad, w_ada_mix, b_ada_mix)
    ada_mlp = _ada(c_pad, w_ada_mlp, b_ada_mlp)
    kv_shift, kv_scale = split(_ada(c_pad, w_ada_kv[None], b_ada_kv[None])[0], 2)

    def row(v):
        return v.reshape(1, -1)

    inv = 1.0 / (ROPE_THETA ** (jnp.arange(0, QK_ROPE_DIM, 2, dtype=F32) / QK_ROPE_DIM))
    inv_col = inv.reshape(-1, 1)
    pos = positions.reshape(1, t)
    q_scale = QK_DIM ** -0.5 * math.log2(math.e)

    h = x.reshape(t, d)
    for l in range(depth):
        shift, scale, gate = split(ada_mix[l], 3)
        if l < n_conv:
            u = _conv_in(h, shift, scale, row(g_pre_mix[l]), conv_w_in, row(conv_b_in[l]), l, seq)
            dw8 = jnp.broadcast_to(conv_dw[l][:, None, :], (CONV_WIDTH, SUBLANES, d))
            h = _conv_out(u, dw8, row(conv_dw_b[l]), row(conv_ln_g[l]), row(conv_ln_b[l]),
                          conv_w_out[l].astype(BF16), row(conv_b_out[l]), h, gate, row(g_post_mix[l]), seq)
        else:
            j = l - n_conv
            w_dkvkr = jnp.concatenate(
                [w_dkv, w_kr, jnp.zeros((d, LANES - QK_ROPE_DIM), F32)], axis=1).astype(BF16)
            qt, k, vt = _proj(h, pos, inv_col, kv_shift, kv_scale, shift, scale, row(g_kv), row(g_pre_mix[l]),
                              w_dkvkr, row(g_ckv), w_uk.astype(BF16), w_uv.T.astype(BF16),
                              w_dq[j].astype(BF16), row(g_cq[j]), w_uq[j].T.astype(BF16), q_scale, batch, seq)
            a = _attention(qt, k, vt)
            h = _attn_out(a, w_o[j].astype(BF16), h, gate, row(g_post_mix[l]), seq)
        shift, scale, gate = split(ada_mlp[l], 3)
        if l == depth - 1:
            h = _mlp(h, shift, scale, gate, row(g_pre_mlp[l]), row(g_post_mlp[l]),
                     mlp_w_up[l:l + 1].astype(BF16), mlp_w_down[l:l + 1].astype(BF16), 0, seq)
        else:
            h = _mlp(h, shift, scale, gate, row(g_pre_mlp[l]), row(g_post_mlp[l]), mlp_w_up, mlp_w_down, l, seq)
    return h.reshape(batch, seq, d)
```

```python
import functools
import math

import jax
import jax.numpy as jnp
from jax import lax
from jax.experimental import pallas as pl
from jax.experimental.pallas import tpu as pltpu

F32 = jnp.float32
BF16 = jnp.bfloat16

EPS = 1e-6
NEG = -1e30
ROPE_THETA = 10000.0

N_HEADS = 16
QK_NOPE_DIM = 128
QK_ROPE_DIM = 64
QK_DIM = QK_NOPE_DIM + QK_ROPE_DIM
V_HEAD_DIM = 128
CONV_WIDTH = 31

LANES = 128
SUBLANES = 8
VMEM_BYTES_V7X = 64 * 1024 * 1024
VMEM_LIMIT = VMEM_BYTES_V7X - 8 * 1024 * 1024

ROW_CHUNK = 16
ROW_GROUP = 4
CONV_ROWS = 32
CONV_LANES = 256
HALO = 32
PROJ_ROWS = 256
ATTN_BLOCK = 512
ATTN_HEADS = 8


def _cparams(*sem, flags=None):
    return pltpu.CompilerParams(dimension_semantics=sem, vmem_limit_bytes=VMEM_LIMIT, flags=flags)


def _dot(a, b):
    return jnp.dot(a, b, preferred_element_type=F32)


def _dot_nt(a, b):
    return lax.dot_general(a, b, (((1,), (1,)), ((), ())), preferred_element_type=F32)


def _sigmoid(x):
    return 1.0 / (1.0 + jnp.exp(-x))


def _row_loop(n_rows, body):
    span = ROW_CHUNK * ROW_GROUP

    def step(i, carry):
        r0 = pl.multiple_of(i * span, span)
        body([pl.ds(r0 + k * ROW_CHUNK, ROW_CHUNK) for k in range(ROW_GROUP)])
        return carry
    lax.fori_loop(0, n_rows // span, step, 0)


def _inv_rms(x):
    return lax.rsqrt(jnp.mean(x * x, axis=-1, keepdims=True) + EPS)


def _ada_kernel(c_ref, *refs):
    *w_refs, b_ref, o_ref = refs
    c = c_ref[...]
    sc = (c * _sigmoid(c)).astype(BF16)
    dk = c.shape[1] // len(w_refs)
    acc = b_ref[0]
    for i, w_ref in enumerate(w_refs):
        acc = acc + _dot(sc[:, i * dk:(i + 1) * dk], w_ref[0].astype(BF16))
    o_ref[0] = acc


def _ada(c_pad, w, b, tn=512, row_splits=1):
    nl, d, n = w.shape
    rows = c_pad.shape[0]
    dk = d // row_splits
    return pl.pallas_call(
        _ada_kernel,
        grid=(nl, n // tn),
        in_specs=[pl.BlockSpec((rows, d), lambda l, j: (0, 0))]
        + [pl.BlockSpec((1, dk, tn), lambda l, j, i=i: (l, i, j)) for i in range(row_splits)]
        + [pl.BlockSpec((1, 1, tn), lambda l, j: (l, 0, j))],
        out_specs=pl.BlockSpec((1, rows, tn), lambda l, j: (l, 0, j)),
        out_shape=jax.ShapeDtypeStruct((nl, rows, n), F32),
        compiler_params=_cparams("arbitrary", "arbitrary"),
        name="ada",
    )(c_pad, *([w] * row_splits), b.reshape(nl, 1, n))


def _ada_rows_kernel(c_ref, w_ref, b_ref, o_ref):
    @pl.when(pl.program_id(1) == 0)
    def _():
        o_ref[0] = jnp.broadcast_to(b_ref[0], o_ref.shape[1:])

    c = c_ref[...]
    o_ref[0] += _dot((c * _sigmoid(c)).astype(BF16), w_ref[0].astype(BF16))


def _ada_rows(c_pad, w, b, tk=256):
    nl, d, n = w.shape
    rows = c_pad.shape[0]
    return pl.pallas_call(
        _ada_rows_kernel,
        grid=(nl, d // tk),
        in_specs=[
            pl.BlockSpec((rows, tk), lambda l, k: (0, k)),
            pl.BlockSpec((1, tk, n), lambda l, k: (l, k, 0)),
            pl.BlockSpec((1, 1, n), lambda l, k: (l, 0, 0)),
        ],
        out_specs=pl.BlockSpec((1, rows, n), lambda l, k: (l, 0, 0)),
        out_shape=jax.ShapeDtypeStruct((nl, rows, n), F32),
        compiler_params=_cparams("arbitrary", "arbitrary"),
        name="ada_rows",
    )(c_pad, w, b.reshape(nl, 1, n))


def _norm_modulate_to(h_ref, g_ref, shift_ref, scale_ref, out_ref, n_rows):
    mul = g_ref[...] * (1.0 + scale_ref[0])
    add = shift_ref[0]

    def body(chunks):
        for rows in chunks:
            x = h_ref[rows, :]
            out_ref[rows, :] = ((x * _inv_rms(x)) * mul + add).astype(BF16)
    _row_loop(n_rows, body)


def _residual_gate_norm(h_ref, y_ref, gate_ref, g_ref, o_ref, n_rows):
    mul = gate_ref[0] * g_ref[...]

    def body(chunks):
        ys = [y_ref[rows, :] for rows in chunks]
        scaled = [y * _inv_rms(y) for y in ys]
        for rows, s in zip(chunks, scaled):
            o_ref[rows, :] = h_ref[rows, :] + s * mul
    _row_loop(n_rows, body)


def _vec_spec(d, seq_tiles):
    return pl.BlockSpec((1, 1, d), lambda m, *_: (m // seq_tiles, 0, 0))


def _row_spec(d):
    return pl.BlockSpec((1, d), lambda *_: (0, 0))


def _conv_in_kernel(h_ref, shift_ref, scale_ref, g_ref, *refs):
    *w_refs, ba_ref, bg_ref, u_ref, hn_ref = refs
    tm, d = h_ref.shape
    bands = len(w_refs) // 2
    dk = d // bands

    @pl.when(pl.program_id(1) == 0)
    def _():
        _norm_modulate_to(h_ref, g_ref, shift_ref, scale_ref, hn_ref, tm)

    def proj(band_refs, bias_ref):
        acc = bias_ref[...]
        for i, w_ref in enumerate(band_refs):
            acc = acc + _dot(hn_ref[:, i * dk:(i + 1) * dk], w_ref[...].astype(BF16))
        return acc

    u_ref[...] = proj(w_refs[:bands], ba_ref) * _sigmoid(proj(w_refs[bands:], bg_ref))


def _conv_in(h, shift, scale, g_pre, w_in, b_in, layer, seq, tm=1024, tn=512, bands=2):
    t, d = h.shape
    nt = d // tn
    dk = d // bands
    return pl.pallas_call(
        _conv_in_kernel,
        grid=(t // tm, nt),
        in_specs=[
            pl.BlockSpec((tm, d), lambda m, n: (m, 0)),
            _vec_spec(d, seq // tm), _vec_spec(d, seq // tm), _row_spec(d),
        ] + [pl.BlockSpec((None, dk, tn), lambda m, n, i=i: (layer, i, n)) for i in range(bands)]
        + [pl.BlockSpec((None, dk, tn), lambda m, n, i=i: (layer, i, n + nt)) for i in range(bands)] + [
            pl.BlockSpec((1, tn), lambda m, n: (0, n)),
            pl.BlockSpec((1, tn), lambda m, n: (0, n + nt)),
        ],
        out_specs=pl.BlockSpec((tm, tn), lambda m, n: (m, n)),
        out_shape=jax.ShapeDtypeStruct((t, d), F32),
        scratch_shapes=[pltpu.VMEM((tm, d), BF16)],
        compiler_params=_cparams("arbitrary", "arbitrary"),
        name="conv_in",
    )(h, shift, scale, g_pre, *([w_in] * (2 * bands)), b_in, b_in)


def _conv_out_kernel(u_ref, halo_ref, dw_ref, dwb_ref, lng_ref, lnb_ref, wout_ref, bout_ref,
                     h_ref, gate_ref, gpost_ref, o_ref, sh_ref, cv_ref, a_ref, *, seq_tiles):
    tm, d = u_ref.shape
    first = (pl.program_id(0) % seq_tiles) == 0
    sh_ref[0, 0:HALO, :] = jnp.where(first, 0.0, halo_ref[...])
    sh_ref[0, HALO:, :] = u_ref[...]

    def shift_step(i, carry):
        r0 = pl.multiple_of(i * SUBLANES, SUBLANES)
        x = sh_ref[0, pl.ds(r0, 2 * SUBLANES), :]
        for b in range(1, SUBLANES):
            sh_ref[b, pl.ds(r0, SUBLANES), :] = x[b:b + SUBLANES]
        return carry
    lax.fori_loop(0, (tm + HALO) // SUBLANES - 1, shift_step, 0)

    base = HALO - (CONV_WIDTH - 1)
    groups = CONV_ROWS // SUBLANES
    taps_by_shift = {}
    for j in range(CONV_WIDTH):
        a, b = divmod(base + j, SUBLANES)
        taps_by_shift.setdefault(b, []).append((a, j))

    def conv_step(i, carry):
        r0 = pl.multiple_of(i * CONV_ROWS, CONV_ROWS)
        for c in range(d // CONV_LANES):
            lanes = slice(c * CONV_LANES, (c + 1) * CONV_LANES)
            accs = [jnp.broadcast_to(dwb_ref[:, lanes], (SUBLANES, CONV_LANES))] * groups
            for b, taps in taps_by_shift.items():
                ws = {j: dw_ref[j, :, lanes] for _, j in taps}
                tiles = [a for a, _ in taps]
                for k in range(min(tiles), max(tiles) + groups):
                    x = sh_ref[b, pl.ds(r0 + k * SUBLANES, SUBLANES), lanes]
                    for a, j in taps:
                        if 0 <= k - a < groups:
                            accs[k - a] = accs[k - a] + x * ws[j]
            for g in range(groups):
                cv_ref[pl.ds(r0 + g * SUBLANES, SUBLANES), lanes] = accs[g]
        return carry
    lax.fori_loop(0, tm // CONV_ROWS, conv_step, 0)

    lng = lng_ref[...]
    lnb = lnb_ref[...]

    def ln_body(chunks):
        for rows in chunks:
            x = cv_ref[rows, :]
            mu = jnp.mean(x, axis=-1, keepdims=True)
            xc = x - mu
            var = jnp.mean(xc * xc, axis=-1, keepdims=True)
            y = (xc * lax.rsqrt(var + EPS)) * lng + lnb
            a_ref[rows, :] = (y * _sigmoid(y)).astype(BF16)
    _row_loop(tm, ln_body)

    cv_ref[...] = _dot(a_ref[...], wout_ref[...]) + bout_ref[...]
    _residual_gate_norm(h_ref, cv_ref, gate_ref, gpost_ref, o_ref, tm)


def _conv_out(u, dw8, dw_b, ln_g, ln_b, w_out, b_out, h, gate, g_post, seq, tm=256):
    t, d = u.shape
    seq_tiles = seq // tm
    halo_blocks = tm // HALO
    return pl.pallas_call(
        functools.partial(_conv_out_kernel, seq_tiles=seq_tiles),
        grid=(t // tm,),
        in_specs=[
            pl.BlockSpec((tm, d), lambda m: (m, 0)),
            pl.BlockSpec((HALO, d), lambda m: (jnp.maximum(m * halo_blocks - 1, 0), 0)),
            pl.BlockSpec(dw8.shape, lambda m: (0, 0, 0), pipeline_mode=pl.Buffered(1)),
            _row_spec(d), _row_spec(d), _row_spec(d),
            pl.BlockSpec((d, d), lambda m: (0, 0), pipeline_mode=pl.Buffered(1)),
            _row_spec(d),
            pl.BlockSpec((tm, d), lambda m: (m, 0)),
            _vec_spec(d, seq_tiles), _row_spec(d),
        ],
        out_specs=pl.BlockSpec((tm, d), lambda m: (m, 0)),
        out_shape=jax.ShapeDtypeStruct((t, d), F32),
        scratch_shapes=[pltpu.VMEM((SUBLANES, tm + HALO, d), F32), pltpu.VMEM((tm, d), F32),
                        pltpu.VMEM((tm, d), BF16)],
        compiler_params=_cparams("arbitrary"),
        name="conv_out",
    )(u, u, dw8, dw_b, ln_g, ln_b, w_out, b_out, h, gate, g_post)


def _mlp_kernel(h_ref, shift_ref, scale_ref, gate_ref, gpre_ref, gpost_ref, *refs):
    *wup_refs, wdown_ref, o_ref, hn_ref = refs
    tm, d = h_ref.shape
    dk = d // len(wup_refs)
    f = pl.program_id(1)

    @pl.when(f == 0)
    def _():
        _norm_modulate_to(h_ref, gpre_ref, shift_ref, scale_ref, hn_ref, tm)
        o_ref[...] = jnp.zeros_like(o_ref)

    up = _dot(hn_ref[:, :dk], wup_refs[0][...].astype(BF16))
    for i, wup_ref in enumerate(wup_refs[1:], 1):
        up = up + _dot(hn_ref[:, i * dk:(i + 1) * dk], wup_ref[...].astype(BF16))
    up = jnp.maximum(up, 0.0)
    o_ref[...] += _dot((up * up).astype(BF16), wdown_ref[...].astype(BF16))

    @pl.when(f == pl.num_programs(1) - 1)
    def _():
        _residual_gate_norm(h_ref, o_ref, gate_ref, gpost_ref, o_ref, tm)


def _mlp(h, shift, scale, gate, g_pre, g_post, w_up, w_down, layer, seq, tm=1024, tf=512, up_splits=1):
    t, d = h.shape
    ff = w_up.shape[2]
    seq_tiles = seq // tm
    dk = d // up_splits
    return pl.pallas_call(
        _mlp_kernel,
        grid=(t // tm, ff // tf),
        in_specs=[
            pl.BlockSpec((tm, d), lambda m, f: (m, 0), pipeline_mode=pl.Buffered(1)),
            _vec_spec(d, seq_tiles), _vec_spec(d, seq_tiles), _vec_spec(d, seq_tiles),
            _row_spec(d), _row_spec(d),
        ] + [pl.BlockSpec((None, dk, tf), lambda m, f, i=i: (layer, i, f)) for i in range(up_splits)] + [
            pl.BlockSpec((None, tf, d), lambda m, f: (layer, f, 0)),
        ],
        out_specs=pl.BlockSpec((tm, d), lambda m, f: (m, 0)),
        out_shape=jax.ShapeDtypeStruct((t, d), F32),
        scratch_shapes=[pltpu.VMEM((tm, d), BF16)],
        compiler_params=_cparams("arbitrary", "arbitrary"),
        name="mlp",
    )(h, shift, scale, gate, g_pre, g_post, *([w_up] * up_splits), w_down)


def _rope_t(x, cos, sin):
    half = QK_ROPE_DIM // 2
    x1, x2 = x[:half], x[half:]
    return x1 * cos - x2 * sin, x2 * cos + x1 * sin


def _proj_kernel(h_ref, pos_ref, inv_ref, kvshift_ref, kvscale_ref, shift_ref, scale_ref,
                 gkv_ref, gpre_ref, wdkv_ref, gckv_ref, wuk_ref, wuvt_ref, wdq_ref, gcq_ref, wuqt_ref,
                 qt_ref, k_ref, vt_ref, kvn_ref, hn_ref, *, q_scale):
    tm = h_ref.shape[0]
    kv_mul = gkv_ref[...] * (1.0 + kvscale_ref[0])
    kv_add = kvshift_ref[0]
    q_mul = gpre_ref[...] * (1.0 + scale_ref[0])
    q_add = shift_ref[0]

    def norm_body(chunks):
        for rows in chunks:
            x = h_ref[rows, :]
            xn = x * _inv_rms(x)
            kvn_ref[rows, :] = (xn * kv_mul + kv_add).astype(BF16)
            hn_ref[rows, :] = (xn * q_mul + q_add).astype(BF16)
    _row_loop(tm, norm_body)

    ang = inv_ref[...] * pos_ref[...].astype(F32)
    cos = jnp.cos(ang)
    sin = jnp.sin(ang)

    r_kv = gckv_ref.shape[1]
    t1 = _dot(kvn_ref[...], wdkv_ref[...])
    ckv = t1[:, :r_kv]
    ckv = ((ckv * _inv_rms(ckv)) * gckv_ref[...]).astype(BF16)
    kr1, kr2 = _rope_t(t1[:, r_kv:].T[:QK_ROPE_DIM], cos, sin)
    k_rope = jnp.concatenate([kr1, kr2, jnp.zeros((LANES - QK_ROPE_DIM, tm), F32)], axis=0).T
    k_rope = k_rope[:, :QK_ROPE_DIM].astype(BF16)
    k_nope = _dot(ckv, wuk_ref[...]).astype(BF16)
    vt = _dot_nt(wuvt_ref[...], ckv).astype(BF16)

    cq = _dot(hn_ref[...], wdq_ref[...])
    cq = ((cq * _inv_rms(cq)) * gcq_ref[...]).astype(BF16)
    qt = _dot_nt(wuqt_ref[...], cq) * q_scale
    for hd in range(N_HEADS):
        r0 = hd * QK_DIM
        q1, q2 = _rope_t(qt[r0 + QK_NOPE_DIM:r0 + QK_DIM], cos, sin)
        qt_ref[0, hd, :QK_NOPE_DIM, :] = qt[r0:r0 + QK_NOPE_DIM].astype(BF16)
        qt_ref[0, hd, QK_NOPE_DIM:QK_NOPE_DIM + QK_ROPE_DIM // 2, :] = q1.astype(BF16)
        qt_ref[0, hd, QK_NOPE_DIM + QK_ROPE_DIM // 2:, :] = q2.astype(BF16)
        k_ref[0, hd, :, :QK_NOPE_DIM] = k_nope[:, hd * QK_NOPE_DIM:(hd + 1) * QK_NOPE_DIM]
        k_ref[0, hd, :, QK_NOPE_DIM:] = k_rope
        vt_ref[0, hd, 0] = vt[hd * V_HEAD_DIM:(hd + 1) * V_HEAD_DIM, :]


def _proj(h, pos, inv_col, kv_shift, kv_scale, shift, scale, g_kv, g_pre, w_dkvkr, g_ckv, w_uk, w_uvt,
          w_dq, g_cq, w_uqt, q_scale, batch, seq):
    t, d = h.shape
    tm = PROJ_ROWS
    seq_tiles = seq // tm
    r_kv = g_ckv.shape[1]
    r_q = g_cq.shape[1]

    def full(a):
        return pl.BlockSpec(a.shape, lambda m: (0,) * a.ndim, pipeline_mode=pl.Buffered(1))

    return pl.pallas_call(
        functools.partial(_proj_kernel, q_scale=q_scale),
        grid=(t // tm,),
        in_specs=[
            pl.BlockSpec((tm, d), lambda m: (m, 0)),
            pl.BlockSpec((1, tm), lambda m: (0, m)),
            full(inv_col),
            _vec_spec(d, seq_tiles), _vec_spec(d, seq_tiles), _vec_spec(d, seq_tiles), _vec_spec(d, seq_tiles),
            _row_spec(d), _row_spec(d),
            full(w_dkvkr), _row_spec(r_kv), full(w_uk), full(w_uvt), full(w_dq), _row_spec(r_q), full(w_uqt),
        ],
        out_specs=[
            pl.BlockSpec((1, N_HEADS, QK_DIM, tm), lambda m: (m // seq_tiles, 0, 0, m % seq_tiles)),
            pl.BlockSpec((1, N_HEADS, tm, QK_DIM), lambda m: (m // seq_tiles, 0, m % seq_tiles, 0)),
            pl.BlockSpec((1, N_HEADS, 1, V_HEAD_DIM, tm), lambda m: (m // seq_tiles, 0, m % seq_tiles, 0, 0)),
        ],
        out_shape=[
            jax.ShapeDtypeStruct((batch, N_HEADS, QK_DIM, seq), BF16),
            jax.ShapeDtypeStruct((batch, N_HEADS, seq, QK_DIM), BF16),
            jax.ShapeDtypeStruct((batch, N_HEADS, seq_tiles, V_HEAD_DIM, tm), BF16),
        ],
        scratch_shapes=[pltpu.VMEM((tm, d), BF16), pltpu.VMEM((tm, d), BF16)],
        compiler_params=_cparams("arbitrary"),
        name="proj",
    )(h, pos, inv_col, kv_shift, kv_scale, shift, scale, g_kv, g_pre, w_dkvkr, g_ckv, w_uk, w_uvt, w_dq, g_cq, w_uqt)


def _attn_kernel(qt_ref, k_ref, vt_ref, o_ref, sa_ref, sb_ref, acc_ref, st_ref, *, blk):
    qi = pl.program_id(2)
    heads = range(qt_ref.shape[1])
    half = vt_ref.shape[-1]
    assert blk == 2 * half

    def scores_to(s_ref, hd, tile):
        k = k_ref[0, hd, pl.ds(pl.multiple_of(tile * half, half), half), :]
        s = _dot(k, qt_ref[0, hd])
        s_ref[hd] = s
        return jnp.max(s, axis=0, keepdims=True)

    def update(s, mx, hd, tile, m, l):
        m_new = jnp.maximum(m, mx)
        alpha = jnp.exp2(m - m_new)
        p = jnp.exp2(s - m_new)
        l = alpha * l + jnp.sum(p, axis=0, keepdims=True)
        acc_ref[hd] = alpha * acc_ref[hd] + _dot(vt_ref[0, hd, tile], p.astype(BF16))
        return m_new, l

    def body(j, carry):
        out = []
        for hd, (mxa, m, l) in zip(heads, carry):
            mxb = scores_to(sb_ref, hd, 2 * j + 1)
            m, l = update(sa_ref[hd], mxa, hd, 2 * j, m, l)
            mxa = scores_to(sa_ref, hd, 2 * j + 2)
            m, l = update(sb_ref[hd], mxb, hd, 2 * j + 1, m, l)
            out.append((mxa, m, l))
        return tuple(out)

    acc_ref[...] = jnp.zeros_like(acc_ref)
    init = tuple((scores_to(sa_ref, hd, 0), jnp.full((1, blk), NEG, F32), jnp.zeros((1, blk), F32))
                 for hd in heads)
    carry = lax.fori_loop(0, qi, body, init)

    key = lax.broadcasted_iota(jnp.int32, (half, blk), 0)
    qry = lax.broadcasted_iota(jnp.int32, (half, blk), 1)
    key_sq = lax.broadcasted_iota(jnp.int32, (half, half), 0)
    qry_sq = lax.broadcasted_iota(jnp.int32, (half, half), 1)
    for hd, (_, m, l) in zip(heads, carry):
        kb = k_ref[0, hd, pl.ds(pl.multiple_of((2 * qi + 1) * half, half), half), :]
        sb = jnp.where(key_sq <= qry_sq, _dot(kb, qt_ref[0, hd, :, half:]), NEG)
        sa = jnp.where(key <= qry, sa_ref[hd], NEG)
        m, l = update(sa, jnp.max(sa, axis=0, keepdims=True), hd, 2 * qi, m, l)
        st_ref[0:1, :] = m
        st_ref[1:2, :] = l
        m_new = jnp.maximum(st_ref[0:1, half:], jnp.max(sb, axis=0, keepdims=True))
        alpha = jnp.exp2(st_ref[0:1, half:] - m_new)
        p = jnp.exp2(sb - m_new)
        st_ref[1:2, half:] = alpha * st_ref[1:2, half:] + jnp.sum(p, axis=0, keepdims=True)
        acc_ref[hd, :, half:] = alpha * acc_ref[hd, :, half:] + _dot(vt_ref[0, hd, 2 * qi + 1], p.astype(BF16))
        o_ref[0, hd] = (acc_ref[hd] / st_ref[1:2, :]).T.astype(BF16)


def _attention(qt, k, vt, blk=ATTN_BLOCK, nh=ATTN_HEADS):
    b, n_heads, s, _ = k.shape
    half = vt.shape[-1]
    return pl.pallas_call(
        functools.partial(_attn_kernel, blk=blk),
        grid=(b, n_heads // nh, s // blk),
        in_specs=[
            pl.BlockSpec((1, nh, QK_DIM, blk), lambda bi, hi, qi: (bi, hi, 0, qi)),
            pl.BlockSpec((1, nh, s, QK_DIM), lambda bi, hi, qi: (bi, hi, 0, 0)),
            pl.BlockSpec((1, nh) + vt.shape[2:], lambda bi, hi, qi: (bi, hi, 0, 0, 0)),
        ],
        out_specs=pl.BlockSpec((1, nh, blk, V_HEAD_DIM), lambda bi, hi, qi: (bi, hi, qi, 0)),
        out_shape=jax.ShapeDtypeStruct((b, n_heads, s, V_HEAD_DIM), BF16),
        scratch_shapes=[pltpu.VMEM((nh, half, blk), F32), pltpu.VMEM((nh, half, blk), F32),
                        pltpu.VMEM((nh, V_HEAD_DIM, blk), F32), pltpu.VMEM((SUBLANES, blk), F32)],
        compiler_params=_cparams("arbitrary", "arbitrary", "arbitrary"),
        name="attn",
    )(qt, k, vt)


def _attn_out_kernel(a_ref, wo_ref, h_ref, gate_ref, gpost_ref, o_ref, cat_ref):
    tm = h_ref.shape[0]
    for hd in range(N_HEADS):
        cat_ref[:, hd * V_HEAD_DIM:(hd + 1) * V_HEAD_DIM] = a_ref[0, hd]
    o_ref[...] = _dot(cat_ref[...], wo_ref[...])
    _residual_gate_norm(h_ref, o_ref, gate_ref, gpost_ref, o_ref, tm)


def _attn_out(a, w_o, h, gate, g_post, seq, tm=512):
    t, d = h.shape
    seq_tiles = seq // tm
    return pl.pallas_call(
        _attn_out_kernel,
        grid=(t // tm,),
        in_specs=[
            pl.BlockSpec((1, N_HEADS, tm, V_HEAD_DIM), lambda m: (m // seq_tiles, 0, m % seq_tiles, 0)),
            pl.BlockSpec(w_o.shape, lambda m: (0, 0), pipeline_mode=pl.Buffered(1)),
            pl.BlockSpec((tm, d), lambda m: (m, 0)),
            _vec_spec(d, seq_tiles), _row_spec(d),
        ],
        out_specs=pl.BlockSpec((tm, d), lambda m: (m, 0)),
        out_shape=jax.ShapeDtypeStruct((t, d), F32),
        scratch_shapes=[pltpu.VMEM((tm, N_HEADS * V_HEAD_DIM), BF16)],
        compiler_params=_cparams("arbitrary"),
        name="attn_out",
    )(a, w_o, h, gate, g_post)


def kernel(x, c, positions, w_ada_mix, b_ada_mix, w_ada_mlp, b_ada_mlp, g_pre_mix, g_post_mix, g_pre_mlp,
           g_post_mlp, conv_w_in, conv_b_in, conv_dw, conv_dw_b, conv_ln_g, conv_ln_b, conv_w_out, conv_b_out,
           w_ada_kv, b_ada_kv, g_kv, w_dkv, g_ckv, w_kr, w_uk, w_uv, w_dq, g_cq, w_uq, w_o, mlp_w_up,
           mlp_w_down):
    batch, seq, d = x.shape
    depth = w_ada_mix.shape[0]
    n_conv = conv_w_in.shape[0]
    t = batch * seq
    if depth - n_conv != 1:
        raise NotImplementedError("exactly one MLA layer reads the shared K/V in this trunk")

    c_pad = jnp.pad(c, ((0, -batch % SUBLANES), (0, 0)))

    def split(m, n):
        return [m[:batch, None, i * d:(i + 1) * d] for i in range(n)]

    ada_mix = _ada_rows(c_pad, w_ada_mix, b_ada_mix)
    ada_mlp = _ada(c_pad, w_ada_mlp, b_ada_mlp, row_splits=2)
    kv_shift, kv_scale = split(_ada(c_pad, w_ada_kv[None], b_ada_kv[None])[0], 2)

    def row(v):
        return v.reshape(1, -1)

    inv = 1.0 / (ROPE_THETA ** (jnp.arange(0, QK_ROPE_DIM, 2, dtype=F32) / QK_ROPE_DIM))
    inv_col = inv.reshape(-1, 1)
    pos = positions.reshape(1, t)
    q_scale = QK_DIM ** -0.5 * math.log2(math.e)

    h = x.reshape(t, d)
    for l in range(depth):
        shift, scale, gate = split(ada_mix[l], 3)
        if l < n_conv:
            u = _conv_in(h, shift, scale, row(g_pre_mix[l]), conv_w_in, row(conv_b_in[l]), l, seq)
            dw8 = jnp.broadcast_to(conv_dw[l][:, None, :], (CONV_WIDTH, SUBLANES, d))
            h = _conv_out(u, dw8, row(conv_dw_b[l]), row(conv_ln_g[l]), row(conv_ln_b[l]),
                          conv_w_out[l].astype(BF16), row(conv_b_out[l]), h, gate, row(g_post_mix[l]), seq)
        else:
            j = l - n_conv
            w_dkvkr = jnp.concatenate(
                [w_dkv, w_kr, jnp.zeros((d, LANES - QK_ROPE_DIM), F32)], axis=1).astype(BF16)
            qt, k, vt = _proj(h, pos, inv_col, kv_shift, kv_scale, shift, scale, row(g_kv), row(g_pre_mix[l]),
                              w_dkvkr, row(g_ckv), w_uk.astype(BF16), w_uv.T.astype(BF16),
                              w_dq[j].astype(BF16), row(g_cq[j]), w_uq[j].T.astype(BF16), q_scale, batch, seq)
            a = _attention(qt, k, vt)
            h = _attn_out(a, w_o[j].astype(BF16), h, gate, row(g_post_mix[l]), seq)
        shift, scale, gate = split(ada_mlp[l], 3)
        h = _mlp(h, shift, scale, gate, row(g_pre_mlp[l]), row(g_post_mlp[l]), mlp_w_up, mlp_w_down, l, seq,
                 up_splits=1 + l)
    return h.reshape(batch, seq, d)
```

```python
import functools
import math

import jax
import jax.numpy as jnp
from jax import lax
from jax.experimental import pallas as pl
from jax.experimental.pallas import tpu as pltpu

F32 = jnp.float32
BF16 = jnp.bfloat16

EPS = 1e-6
NEG = -1e30
ROPE_THETA = 10000.0

N_HEADS = 16
QK_NOPE_DIM = 128
QK_ROPE_DIM = 64
QK_DIM = QK_NOPE_DIM + QK_ROPE_DIM
V_HEAD_DIM = 128
CONV_WIDTH = 31

LANES = 128
SUBLANES = 8
VMEM_BYTES_V7X = 64 * 1024 * 1024
VMEM_LIMIT = VMEM_BYTES_V7X - 8 * 1024 * 1024

ROW_CHUNK = 16
ROW_GROUP = 16
CONV_ROWS = 32
CONV_LANES = 256
HALO = 32
PROJ_ROWS = 256
ATTN_BLOCK = 512
ATTN_HEADS = 8


def _cparams(*sem, flags=None):
    return pltpu.CompilerParams(dimension_semantics=sem, vmem_limit_bytes=VMEM_LIMIT, flags=flags)


def _dot(a, b):
    return jnp.dot(a, b, preferred_element_type=F32)


def _dot_nt(a, b):
    return lax.dot_general(a, b, (((1,), (1,)), ((), ())), preferred_element_type=F32)


def _sigmoid(x):
    return 1.0 / (1.0 + jnp.exp(-x))


def _row_loop(n_rows, body):
    span = ROW_CHUNK * ROW_GROUP

    def step(i, carry):
        r0 = pl.multiple_of(i * span, span)
        body([pl.ds(r0 + k * ROW_CHUNK, ROW_CHUNK) for k in range(ROW_GROUP)])
        return carry
    lax.fori_loop(0, n_rows // span, step, 0)


def _inv_rms(x):
    return lax.rsqrt(jnp.mean(x * x, axis=-1, keepdims=True) + EPS)


def _ada_kernel(c_ref, w_ref, b_ref, o_ref):
    @pl.when(pl.program_id(1) == 0)
    def _():
        o_ref[0] = jnp.broadcast_to(b_ref[0], o_ref.shape[1:])

    c = c_ref[...]
    o_ref[0] += _dot((c * _sigmoid(c)).astype(BF16), w_ref[0].astype(BF16))


def _ada(c_pad, w, b, tk=256):
    nl, d, n = w.shape
    rows = c_pad.shape[0]
    return pl.pallas_call(
        _ada_kernel,
        grid=(nl, d // tk),
        in_specs=[
            pl.BlockSpec((rows, tk), lambda l, k: (0, k)),
            pl.BlockSpec((1, tk, n), lambda l, k: (l, k, 0)),
            pl.BlockSpec((1, 1, n), lambda l, k: (l, 0, 0)),
        ],
        out_specs=pl.BlockSpec((1, rows, n), lambda l, k: (l, 0, 0)),
        out_shape=jax.ShapeDtypeStruct((nl, rows, n), F32),
        compiler_params=_cparams("arbitrary", "arbitrary"),
        name="ada",
    )(c_pad, w, b.reshape(nl, 1, n))


def _norm_modulate_to(h_ref, g_ref, shift_ref, scale_ref, out_ref, n_rows):
    mul = g_ref[...] * (1.0 + scale_ref[0])
    add = shift_ref[0]

    def body(chunks):
        for rows in chunks:
            x = h_ref[rows, :]
            out_ref[rows, :] = ((x * _inv_rms(x)) * mul + add).astype(BF16)
    _row_loop(n_rows, body)


def _residual_gate_norm(h_ref, y_ref, gate_ref, g_ref, o_ref, n_rows):
    mul = gate_ref[0] * g_ref[...]

    def body(chunks):
        ys = [y_ref[rows, :] for rows in chunks]
        scaled = [y * _inv_rms(y) for y in ys]
        for rows, s in zip(chunks, scaled):
            o_ref[rows, :] = h_ref[rows, :] + s * mul
    _row_loop(n_rows, body)


def _vec_spec(d, seq_tiles):
    return pl.BlockSpec((1, 1, d), lambda m, *_: (m // seq_tiles, 0, 0))


def _row_spec(d):
    return pl.BlockSpec((1, d), lambda *_: (0, 0))


def _conv_in_kernel(h_ref, shift_ref, scale_ref, g_ref, wa_ref, wg_ref, ba_ref, bg_ref, u_ref, hn_ref):
    tm = h_ref.shape[0]

    @pl.when(pl.program_id(1) == 0)
    def _():
        _norm_modulate_to(h_ref, g_ref, shift_ref, scale_ref, hn_ref, tm)

    hn = hn_ref[...]
    a = _dot(hn, wa_ref[...].astype(BF16)) + ba_ref[...]
    g = _dot(hn, wg_ref[...].astype(BF16)) + bg_ref[...]
    u_ref[...] = a * _sigmoid(g)


def _conv_in(h, shift, scale, g_pre, w_in, b_in, layer, seq, tm=1024, tn=512):
    t, d = h.shape
    nt = d // tn
    return pl.pallas_call(
        _conv_in_kernel,
        grid=(t // tm, nt),
        in_specs=[
            pl.BlockSpec((tm, d), lambda m, n: (m, 0)),
            _vec_spec(d, seq // tm), _vec_spec(d, seq // tm), _row_spec(d),
            pl.BlockSpec((None, d, tn), lambda m, n: (layer, 0, n)),
            pl.BlockSpec((None, d, tn), lambda m, n: (layer, 0, n + nt)),
            pl.BlockSpec((1, tn), lambda m, n: (0, n)),
            pl.BlockSpec((1, tn), lambda m, n: (0, n + nt)),
        ],
        out_specs=pl.BlockSpec((tm, tn), lambda m, n: (m, n)),
        out_shape=jax.ShapeDtypeStruct((t, d), F32),
        scratch_shapes=[pltpu.VMEM((tm, d), BF16)],
        compiler_params=_cparams("arbitrary", "arbitrary"),
        name="conv_in",
    )(h, shift, scale, g_pre, w_in, w_in, b_in, b_in)


def _conv_out_kernel(u_ref, halo_ref, dw_ref, dwb_ref, lng_ref, lnb_ref, wout_ref, bout_ref,
                     h_ref, gate_ref, gpost_ref, o_ref, sh_ref, cv_ref, a_ref, *, seq_tiles):
    tm, d = u_ref.shape
    first = (pl.program_id(0) % seq_tiles) == 0
    sh_ref[0, 0:HALO, :] = jnp.where(first, 0.0, halo_ref[...])
    sh_ref[0, HALO:, :] = u_ref[...]

    def shift_step(i, carry):
        r0 = pl.multiple_of(i * SUBLANES, SUBLANES)
        x = sh_ref[0, pl.ds(r0, 2 * SUBLANES), :]
        for b in range(1, SUBLANES):
            sh_ref[b, pl.ds(r0, SUBLANES), :] = x[b:b + SUBLANES]
        return carry
    lax.fori_loop(0, (tm + HALO) // SUBLANES - 1, shift_step, 0)

    base = HALO - (CONV_WIDTH - 1)
    groups = CONV_ROWS // SUBLANES
    taps_by_shift = {}
    for j in range(CONV_WIDTH):
        a, b = divmod(base + j, SUBLANES)
        taps_by_shift.setdefault(b, []).append((a, j))

    def conv_step(i, carry):
        r0 = pl.multiple_of(i * CONV_ROWS, CONV_ROWS)
        for c in range(d // CONV_LANES):
            lanes = slice(c * CONV_LANES, (c + 1) * CONV_LANES)
            accs = [jnp.broadcast_to(dwb_ref[:, lanes], (SUBLANES, CONV_LANES))] * groups
            for b, taps in taps_by_shift.items():
                ws = {j: dw_ref[j, :, lanes] for _, j in taps}
                tiles = [a for a, _ in taps]
                for k in range(min(tiles), max(tiles) + groups):
                    x = sh_ref[b, pl.ds(r0 + k * SUBLANES, SUBLANES), lanes]
                    for a, j in taps:
                        if 0 <= k - a < groups:
                            accs[k - a] = accs[k - a] + x * ws[j]
            for g in range(groups):
                cv_ref[pl.ds(r0 + g * SUBLANES, SUBLANES), lanes] = accs[g]
        return carry
    lax.fori_loop(0, tm // CONV_ROWS, conv_step, 0)

    lng = lng_ref[...]
    lnb = lnb_ref[...]

    def ln_body(chunks):
        for rows in chunks:
            x = cv_ref[rows, :]
            mu = jnp.mean(x, axis=-1, keepdims=True)
            xc = x - mu
            var = jnp.mean(xc * xc, axis=-1, keepdims=True)
            y = (xc * lax.rsqrt(var + EPS)) * lng + lnb
            a_ref[rows, :] = (y * _sigmoid(y)).astype(BF16)
    _row_loop(tm, ln_body)

    cv_ref[...] = _dot(a_ref[...], wout_ref[...]) + bout_ref[...]
    _residual_gate_norm(h_ref, cv_ref, gate_ref, gpost_ref, o_ref, tm)


def _conv_out(u, dw8, dw_b, ln_g, ln_b, w_out, b_out, h, gate, g_post, seq, tm=256):
    t, d = u.shape
    seq_tiles = seq // tm
    halo_blocks = tm // HALO
    return pl.pallas_call(
        functools.partial(_conv_out_kernel, seq_tiles=seq_tiles),
        grid=(t // tm,),
        in_specs=[
            pl.BlockSpec((tm, d), lambda m: (m, 0)),
            pl.BlockSpec((HALO, d), lambda m: (jnp.maximum(m * halo_blocks - 1, 0), 0)),
            pl.BlockSpec(dw8.shape, lambda m: (0, 0, 0), pipeline_mode=pl.Buffered(1)),
            _row_spec(d), _row_spec(d), _row_spec(d),
            pl.BlockSpec((d, d), lambda m: (0, 0), pipeline_mode=pl.Buffered(1)),
            _row_spec(d),
            pl.BlockSpec((tm, d), lambda m: (m, 0)),
            _vec_spec(d, seq_tiles), _row_spec(d),
        ],
        out_specs=pl.BlockSpec((tm, d), lambda m: (m, 0)),
        out_shape=jax.ShapeDtypeStruct((t, d), F32),
        scratch_shapes=[pltpu.VMEM((SUBLANES, tm + HALO, d), F32), pltpu.VMEM((tm, d), F32),
                        pltpu.VMEM((tm, d), BF16)],
        compiler_params=_cparams("arbitrary"),
        name="conv_out",
    )(u, u, dw8, dw_b, ln_g, ln_b, w_out, b_out, h, gate, g_post)


def _mlp_kernel(h_ref, shift_ref, scale_ref, gate_ref, gpre_ref, gpost_ref, wup_ref, wdown_ref,
                o_ref, hn_ref):
    tm = h_ref.shape[0]
    f = pl.program_id(1)

    @pl.when(f == 0)
    def _():
        _norm_modulate_to(h_ref, gpre_ref, shift_ref, scale_ref, hn_ref, tm)
        o_ref[...] = jnp.zeros_like(o_ref)

    up = jnp.maximum(_dot(hn_ref[...], wup_ref[...].astype(BF16)), 0.0)
    o_ref[...] += _dot((up * up).astype(BF16), wdown_ref[...].astype(BF16))

    @pl.when(f == pl.num_programs(1) - 1)
    def _():
        _residual_gate_norm(h_ref, o_ref, gate_ref, gpost_ref, o_ref, tm)


def _mlp(h, shift, scale, gate, g_pre, g_post, w_up, w_down, layer, seq, tm=1024, tf=512):
    t, d = h.shape
    ff = w_up.shape[2]
    seq_tiles = seq // tm
    return pl.pallas_call(
        _mlp_kernel,
        grid=(t // tm, ff // tf),
        in_specs=[
            pl.BlockSpec((tm, d), lambda m, f: (m, 0), pipeline_mode=pl.Buffered(1)),
            _vec_spec(d, seq_tiles), _vec_spec(d, seq_tiles), _vec_spec(d, seq_tiles),
            _row_spec(d), _row_spec(d),
            pl.BlockSpec((None, d, tf), lambda m, f: (layer, 0, f)),
            pl.BlockSpec((None, tf, d), lambda m, f: (layer, f, 0)),
        ],
        out_specs=pl.BlockSpec((tm, d), lambda m, f: (m, 0)),
        out_shape=jax.ShapeDtypeStruct((t, d), F32),
        scratch_shapes=[pltpu.VMEM((tm, d), BF16)],
        compiler_params=_cparams("arbitrary", "arbitrary"),
        name="mlp",
    )(h, shift, scale, gate, g_pre, g_post, w_up, w_down)


def _rope_t(x, cos, sin):
    half = QK_ROPE_DIM // 2
    x1, x2 = x[:half], x[half:]
    return x1 * cos - x2 * sin, x2 * cos + x1 * sin


def _proj_kernel(h_ref, pos_ref, inv_ref, kvshift_ref, kvscale_ref, shift_ref, scale_ref,
                 gkv_ref, gpre_ref, wdkv_ref, gckv_ref, wuk_ref, wuvt_ref, wdq_ref, gcq_ref, wuqt_ref,
                 qt_ref, k_ref, vt_ref, kvn_ref, hn_ref, *, q_scale):
    tm = h_ref.shape[0]
    kv_mul = gkv_ref[...] * (1.0 + kvscale_ref[0])
    kv_add = kvshift_ref[0]
    q_mul = gpre_ref[...] * (1.0 + scale_ref[0])
    q_add = shift_ref[0]

    def norm_body(chunks):
        for rows in chunks:
            x = h_ref[rows, :]
            xn = x * _inv_rms(x)
            kvn_ref[rows, :] = (xn * kv_mul + kv_add).astype(BF16)
            hn_ref[rows, :] = (xn * q_mul + q_add).astype(BF16)
    _row_loop(tm, norm_body)

    ang = inv_ref[...] * pos_ref[...].astype(F32)
    cos = jnp.cos(ang)
    sin = jnp.sin(ang)

    r_kv = gckv_ref.shape[1]
    t1 = _dot(kvn_ref[...], wdkv_ref[...])
    ckv = t1[:, :r_kv]
    ckv = ((ckv * _inv_rms(ckv)) * gckv_ref[...]).astype(BF16)
    kr1, kr2 = _rope_t(t1[:, r_kv:].T[:QK_ROPE_DIM], cos, sin)
    k_rope = jnp.concatenate([kr1, kr2, jnp.zeros((LANES - QK_ROPE_DIM, tm), F32)], axis=0).T
    k_rope = k_rope[:, :QK_ROPE_DIM].astype(BF16)
    k_nope = _dot(ckv, wuk_ref[...]).astype(BF16)
    vt = _dot_nt(wuvt_ref[...], ckv).astype(BF16)

    cq = _dot(hn_ref[...], wdq_ref[...])
    cq = ((cq * _inv_rms(cq)) * gcq_ref[...]).astype(BF16)
    qt = _dot_nt(wuqt_ref[...], cq) * q_scale
    for hd in range(N_HEADS):
        r0 = hd * QK_DIM
        q1, q2 = _rope_t(qt[r0 + QK_NOPE_DIM:r0 + QK_DIM], cos, sin)
        qt_ref[0, hd, :QK_NOPE_DIM, :] = qt[r0:r0 + QK_NOPE_DIM].astype(BF16)
        qt_ref[0, hd, QK_NOPE_DIM:QK_NOPE_DIM + QK_ROPE_DIM // 2, :] = q1.astype(BF16)
        qt_ref[0, hd, QK_NOPE_DIM + QK_ROPE_DIM // 2:, :] = q2.astype(BF16)
        k_ref[0, hd, :, :QK_NOPE_DIM] = k_nope[:, hd * QK_NOPE_DIM:(hd + 1) * QK_NOPE_DIM]
        k_ref[0, hd, :, QK_NOPE_DIM:] = k_rope
        vt_ref[0, hd, 0] = vt[hd * V_HEAD_DIM:(hd + 1) * V_HEAD_DIM, :]


def _proj(h, pos, inv_col, kv_shift, kv_scale, shift, scale, g_kv, g_pre, w_dkvkr, g_ckv, w_uk, w_uvt,
          w_dq, g_cq, w_uqt, q_scale, batch, seq):
    t, d = h.shape
    tm = PROJ_ROWS
    seq_tiles = seq // tm
    r_kv = g_ckv.shape[1]
    r_q = g_cq.shape[1]

    def full(a):
        return pl.BlockSpec(a.shape, lambda m: (0,) * a.ndim, pipeline_mode=pl.Buffered(1))

    return pl.pallas_call(
        functools.partial(_proj_kernel, q_scale=q_scale),
        grid=(t // tm,),
        in_specs=[
            pl.BlockSpec((tm, d), lambda m: (m, 0)),
            pl.BlockSpec((1, tm), lambda m: (0, m)),
            full(inv_col),
            _vec_spec(d, seq_tiles), _vec_spec(d, seq_tiles), _vec_spec(d, seq_tiles), _vec_spec(d, seq_tiles),
            _row_spec(d), _row_spec(d),
            full(w_dkvkr), _row_spec(r_kv), full(w_uk), full(w_uvt), full(w_dq), _row_spec(r_q), full(w_uqt),
        ],
        out_specs=[
            pl.BlockSpec((1, N_HEADS, QK_DIM, tm), lambda m: (m // seq_tiles, 0, 0, m % seq_tiles)),
            pl.BlockSpec((1, N_HEADS, tm, QK_DIM), lambda m: (m // seq_tiles, 0, m % seq_tiles, 0)),
            pl.BlockSpec((1, N_HEADS, 1, V_HEAD_DIM, tm), lambda m: (m // seq_tiles, 0, m % seq_tiles, 0, 0)),
        ],
        out_shape=[
            jax.ShapeDtypeStruct((batch, N_HEADS, QK_DIM, seq), BF16),
            jax.ShapeDtypeStruct((batch, N_HEADS, seq, QK_DIM), BF16),
            jax.ShapeDtypeStruct((batch, N_HEADS, seq_tiles, V_HEAD_DIM, tm), BF16),
        ],
        scratch_shapes=[pltpu.VMEM((tm, d), BF16), pltpu.VMEM((tm, d), BF16)],
        compiler_params=_cparams("arbitrary"),
        name="proj",
    )(h, pos, inv_col, kv_shift, kv_scale, shift, scale, g_kv, g_pre, w_dkvkr, g_ckv, w_uk, w_uvt, w_dq, g_cq, w_uqt)


def _attn_kernel(qt_ref, k_ref, vt_ref, o_ref, sa_ref, sb_ref, acc_ref, st_ref, *, blk):
    qi = pl.program_id(2)
    heads = range(qt_ref.shape[1])
    half = vt_ref.shape[-1]
    assert blk == 2 * half

    def scores_to(s_ref, hd, tile):
        k = k_ref[0, hd, pl.ds(pl.multiple_of(tile * half, half), half), :]
        s = _dot(k, qt_ref[0, hd])
        s_ref[hd] = s
        return jnp.max(s, axis=0, keepdims=True)

    def update(s, mx, hd, tile, m, l):
        m_new = jnp.maximum(m, mx)
        alpha = jnp.exp2(m - m_new)
        p = jnp.exp2(s - m_new)
        l = alpha * l + jnp.sum(p, axis=0, keepdims=True)
        acc_ref[hd] = alpha * acc_ref[hd] + _dot(vt_ref[0, hd, tile], p.astype(BF16))
        return m_new, l

    def body(j, carry):
        out = []
        for hd, (mxa, m, l) in zip(heads, carry):
            mxb = scores_to(sb_ref, hd, 2 * j + 1)
            m, l = update(sa_ref[hd], mxa, hd, 2 * j, m, l)
            mxa = scores_to(sa_ref, hd, 2 * j + 2)
            m, l = update(sb_ref[hd], mxb, hd, 2 * j + 1, m, l)
            out.append((mxa, m, l))
        return tuple(out)

    acc_ref[...] = jnp.zeros_like(acc_ref)
    init = tuple((scores_to(sa_ref, hd, 0), jnp.full((1, blk), NEG, F32), jnp.zeros((1, blk), F32))
                 for hd in heads)
    carry = lax.fori_loop(0, qi, body, init)

    key = lax.broadcasted_iota(jnp.int32, (half, blk), 0)
    qry = lax.broadcasted_iota(jnp.int32, (half, blk), 1)
    key_sq = lax.broadcasted_iota(jnp.int32, (half, half), 0)
    qry_sq = lax.broadcasted_iota(jnp.int32, (half, half), 1)
    for hd, (_, m, l) in zip(heads, carry):
        kb = k_ref[0, hd, pl.ds(pl.multiple_of((2 * qi + 1) * half, half), half), :]
        sb = jnp.where(key_sq <= qry_sq, _dot(kb, qt_ref[0, hd, :, half:]), NEG)
        sa = jnp.where(key <= qry, sa_ref[hd], NEG)
        m, l = update(sa, jnp.max(sa, axis=0, keepdims=True), hd, 2 * qi, m, l)
        st_ref[0:1, :] = m
        st_ref[1:2, :] = l
        m_new = jnp.maximum(st_ref[0:1, half:], jnp.max(sb, axis=0, keepdims=True))
        alpha = jnp.exp2(st_ref[0:1, half:] - m_new)
        p = jnp.exp2(sb - m_new)
        st_ref[1:2, half:] = alpha * st_ref[1:2, half:] + jnp.sum(p, axis=0, keepdims=True)
        acc_ref[hd, :, half:] = alpha * acc_ref[hd, :, half:] + _dot(vt_ref[0, hd, 2 * qi + 1], p.astype(BF16))
        o_ref[0, hd] = (acc_ref[hd] / st_ref[1:2, :]).T.astype(BF16)


def _attention(qt, k, vt, blk=ATTN_BLOCK, nh=ATTN_HEADS):
    b, n_heads, s, _ = k.shape
    half = vt.shape[-1]
    return pl.pallas_call(
        functools.partial(_attn_kernel, blk=blk),
        grid=(b, n_heads // nh, s // blk),
        in_specs=[
            pl.BlockSpec((1, nh, QK_DIM, blk), lambda bi, hi, qi: (bi, hi, 0, qi)),
            pl.BlockSpec((1, nh, s, QK_DIM), lambda bi, hi, qi: (bi, hi, 0, 0)),
            pl.BlockSpec((1, nh) + vt.shape[2:], lambda bi, hi, qi: (bi, hi, 0, 0, 0)),
        ],
        out_specs=pl.BlockSpec((1, nh, blk, V_HEAD_DIM), lambda bi, hi, qi: (bi, hi, qi, 0)),
        out_shape=jax.ShapeDtypeStruct((b, n_heads, s, V_HEAD_DIM), BF16),
        scratch_shapes=[pltpu.VMEM((nh, half, blk), F32), pltpu.VMEM((nh, half, blk), F32),
                        pltpu.VMEM((nh, V_HEAD_DIM, blk), F32), pltpu.VMEM((SUBLANES, blk), F32)],
        compiler_params=_cparams("arbitrary", "arbitrary", "arbitrary"),
        name="attn",
    )(qt, k, vt)


def _attn_out_kernel(a_ref, wo_ref, h_ref, gate_ref, gpost_ref, o_ref, cat_ref):
    tm = h_ref.shape[0]
    for hd in range(N_HEADS):
        cat_ref[:, hd * V_HEAD_DIM:(hd + 1) * V_HEAD_DIM] = a_ref[0, hd]
    o_ref[...] = _dot(cat_ref[...], wo_ref[...])
    _residual_gate_norm(h_ref, o_ref, gate_ref, gpost_ref, o_ref, tm)


def _attn_out(a, w_o, h, gate, g_post, seq, tm=256):
    t, d = h.shape
    seq_tiles = seq // tm
    return pl.pallas_call(
        _attn_out_kernel,
        grid=(t // tm,),
        in_specs=[
            pl.BlockSpec((1, N_HEADS, tm, V_HEAD_DIM), lambda m: (m // seq_tiles, 0, m % seq_tiles, 0)),
            pl.BlockSpec(w_o.shape, lambda m: (0, 0), pipeline_mode=pl.Buffered(1)),
            pl.BlockSpec((tm, d), lambda m: (m, 0)),
            _vec_spec(d, seq_tiles), _row_spec(d),
        ],
        out_specs=pl.BlockSpec((tm, d), lambda m: (m, 0)),
        out_shape=jax.ShapeDtypeStruct((t, d), F32),
        scratch_shapes=[pltpu.VMEM((tm, N_HEADS * V_HEAD_DIM), BF16)],
        compiler_params=_cparams("arbitrary"),
        name="attn_out",
    )(a, w_o, h, gate, g_post)


def kernel(x, c, positions, w_ada_mix, b_ada_mix, w_ada_mlp, b_ada_mlp, g_pre_mix, g_post_mix, g_pre_mlp,
           g_post_mlp, conv_w_in, conv_b_in, conv_dw, conv_dw_b, conv_ln_g, conv_ln_b, conv_w_out, conv_b_out,
           w_ada_kv, b_ada_kv, g_kv, w_dkv, g_ckv, w_kr, w_uk, w_uv, w_dq, g_cq, w_uq, w_o, mlp_w_up,
           mlp_w_down):
    batch, seq, d = x.shape
    depth = w_ada_mix.shape[0]
    n_conv = conv_w_in.shape[0]
    t = batch * seq
    if depth - n_conv != 1:
        raise NotImplementedError("exactly one MLA layer reads the shared K/V in this trunk")

    c_pad = jnp.pad(c, ((0, -batch % SUBLANES), (0, 0)))

    def split(m, n):
        return [m[:batch, None, i * d:(i + 1) * d] for i in range(n)]

    ada_mix = _ada(c_pad, w_ada_mix, b_ada_mix)
    ada_mlp = _ada(c_pad, w_ada_mlp, b_ada_mlp)
    kv_shift, kv_scale = split(_ada(c_pad, w_ada_kv[None], b_ada_kv[None])[0], 2)

    def row(v):
        return v.reshape(1, -1)

    inv = 1.0 / (ROPE_THETA ** (jnp.arange(0, QK_ROPE_DIM, 2, dtype=F32) / QK_ROPE_DIM))
    inv_col = inv.reshape(-1, 1)
    pos = positions.reshape(1, t)
    q_scale = QK_DIM ** -0.5 * math.log2(math.e)

    h = x.reshape(t, d)
    for l in range(depth):
        shift, scale, gate = split(ada_mix[l], 3)
        if l < n_conv:
            u = _conv_in(h, shift, scale, row(g_pre_mix[l]), conv_w_in, row(conv_b_in[l]), l, seq)
            dw8 = jnp.broadcast_to(conv_dw[l][:, None, :], (CONV_WIDTH, SUBLANES, d))
            h = _conv_out(u, dw8, row(conv_dw_b[l]), row(conv_ln_g[l]), row(conv_ln_b[l]),
                          conv_w_out[l].astype(BF16), row(conv_b_out[l]), h, gate, row(g_post_mix[l]), seq)
        else:
            j = l - n_conv
            w_dkvkr = jnp.concatenate(
                [w_dkv, w_kr, jnp.zeros((d, LANES - QK_ROPE_DIM), F32)], axis=1).astype(BF16)
            qt, k, vt = _proj(h, pos, inv_col, kv_shift, kv_scale, shift, scale, row(g_kv), row(g_pre_mix[l]),
                              w_dkvkr, row(g_ckv), w_uk.astype(BF16), w_uv.T.astype(BF16),
                              w_dq[j].astype(BF16), row(g_cq[j]), w_uq[j].T.astype(BF16), q_scale, batch, seq)
            a = _attention(qt, k, vt)
            h = _attn_out(a, w_o[j].astype(BF16), h, gate, row(g_post_mix[l]), seq)
        shift, scale, gate = split(ada_mlp[l], 3)
        h = _mlp(h, shift, scale, gate, row(g_pre_mlp[l]), row(g_post_mlp[l]), mlp_w_up, mlp_w_down, l, seq)
    return h.reshape(batch, seq, d)
```

```python
import functools
import math

import jax
import jax.numpy as jnp
from jax import lax
from jax.experimental import pallas as pl
from jax.experimental.pallas import tpu as pltpu

F32 = jnp.float32
BF16 = jnp.bfloat16

EPS = 1e-6
NEG = -1e30
ROPE_THETA = 10000.0

N_HEADS = 16
QK_NOPE_DIM = 128
QK_ROPE_DIM = 64
QK_DIM = QK_NOPE_DIM + QK_ROPE_DIM
V_HEAD_DIM = 128
CONV_WIDTH = 31

LANES = 128
SUBLANES = 8
VMEM_BYTES_V7X = 64 * 1024 * 1024
VMEM_LIMIT = VMEM_BYTES_V7X - 8 * 1024 * 1024

ROW_CHUNK = 16
ROW_GROUP = 16
CONV_ROWS = 32
CONV_LANES = 256
HALO = 32
MLP_TAIL_ROWS = 256
PROJ_ROWS = 256
ATTN_BLOCK = 512
ATTN_HEADS = 8


def _cparams(*sem, flags=None):
    return pltpu.CompilerParams(dimension_semantics=sem, vmem_limit_bytes=VMEM_LIMIT, flags=flags)


def _dot(a, b):
    return jnp.dot(a, b, preferred_element_type=F32)


def _dot_nt(a, b):
    return lax.dot_general(a, b, (((1,), (1,)), ((), ())), preferred_element_type=F32)


def _sigmoid(x):
    return 1.0 / (1.0 + jnp.exp(-x))


def _row_loop(n_rows, body, row0=0, inline=False):
    span = ROW_CHUNK * ROW_GROUP
    if inline:
        for r0 in range(row0, row0 + n_rows, span):
            body([pl.ds(r0 + k * ROW_CHUNK, ROW_CHUNK) for k in range(ROW_GROUP)])
        return

    def step(i, carry):
        r0 = pl.multiple_of(row0 + i * span, span)
        body([pl.ds(r0 + k * ROW_CHUNK, ROW_CHUNK) for k in range(ROW_GROUP)])
        return carry
    lax.fori_loop(0, n_rows // span, step, 0)


def _inv_rms(x):
    return lax.rsqrt(jnp.mean(x * x, axis=-1, keepdims=True) + EPS)


def _ada_kernel(c_ref, w_ref, b_ref, o_ref):
    @pl.when(pl.program_id(1) == 0)
    def _():
        o_ref[0] = jnp.broadcast_to(b_ref[0], o_ref.shape[1:])

    c = c_ref[...]
    o_ref[0] += _dot((c * _sigmoid(c)).astype(BF16), w_ref[0].astype(BF16))


def _ada(c_pad, w, b, tk=256):
    nl, d, n = w.shape
    rows = c_pad.shape[0]
    return pl.pallas_call(
        _ada_kernel,
        grid=(nl, d // tk),
        in_specs=[
            pl.BlockSpec((rows, tk), lambda l, k: (0, k)),
            pl.BlockSpec((1, tk, n), lambda l, k: (l, k, 0)),
            pl.BlockSpec((1, 1, n), lambda l, k: (l, 0, 0)),
        ],
        out_specs=pl.BlockSpec((1, rows, n), lambda l, k: (l, 0, 0)),
        out_shape=jax.ShapeDtypeStruct((nl, rows, n), F32),
        compiler_params=_cparams("arbitrary", "arbitrary"),
        name="ada",
    )(c_pad, w, b.reshape(nl, 1, n))


def _norm_modulate_to(h_ref, g_ref, shift_ref, scale_ref, out_ref, n_rows, **loop_kw):
    mul = g_ref[...] * (1.0 + scale_ref[0])
    add = shift_ref[0]

    def body(chunks):
        for rows in chunks:
            x = h_ref[rows, :]
            out_ref[rows, :] = ((x * _inv_rms(x)) * mul + add).astype(BF16)
    _row_loop(n_rows, body, **loop_kw)


def _residual_gate_norm(h_ref, y_ref, gate_ref, g_ref, o_ref, n_rows, **loop_kw):
    mul = gate_ref[0] * g_ref[...]

    def body(chunks):
        ys = [y_ref[rows, :] for rows in chunks]
        scaled = [y * _inv_rms(y) for y in ys]
        for rows, s in zip(chunks, scaled):
            o_ref[rows, :] = h_ref[rows, :] + s * mul
    _row_loop(n_rows, body, **loop_kw)


def _vec_spec(d, seq_tiles):
    return pl.BlockSpec((1, 1, d), lambda m, *_: (m // seq_tiles, 0, 0))


def _row_spec(d):
    return pl.BlockSpec((1, d), lambda *_: (0, 0))


def _conv_in_kernel(h_ref, shift_ref, scale_ref, g_ref, wa_ref, wg_ref, ba_ref, bg_ref, u_ref, hn_ref):
    tm = h_ref.shape[0]

    def glu():
        hn = hn_ref[...]
        a = _dot(hn, wa_ref[...].astype(BF16)) + ba_ref[...]
        g = _dot(hn, wg_ref[...].astype(BF16)) + bg_ref[...]
        u_ref[...] = a * _sigmoid(g)

    @pl.when(pl.program_id(1) == 0)
    def _():
        _norm_modulate_to(h_ref, g_ref, shift_ref, scale_ref, hn_ref, tm, inline=True)
        glu()

    @pl.when(pl.program_id(1) != 0)
    def _():
        glu()


def _conv_in(h, shift, scale, g_pre, w_in, b_in, layer, seq, tm=1024, tn=512):
    t, d = h.shape
    nt = d // tn
    return pl.pallas_call(
        _conv_in_kernel,
        grid=(t // tm, nt),
        in_specs=[
            pl.BlockSpec((tm, d), lambda m, n: (m, 0)),
            _vec_spec(d, seq // tm), _vec_spec(d, seq // tm), _row_spec(d),
            pl.BlockSpec((None, d, tn), lambda m, n: (layer, 0, n)),
            pl.BlockSpec((None, d, tn), lambda m, n: (layer, 0, n + nt)),
            pl.BlockSpec((1, tn), lambda m, n: (0, n)),
            pl.BlockSpec((1, tn), lambda m, n: (0, n + nt)),
        ],
        out_specs=pl.BlockSpec((tm, tn), lambda m, n: (m, n)),
        out_shape=jax.ShapeDtypeStruct((t, d), F32),
        scratch_shapes=[pltpu.VMEM((tm, d), BF16)],
        compiler_params=_cparams("arbitrary", "arbitrary"),
        name="conv_in",
    )(h, shift, scale, g_pre, w_in, w_in, b_in, b_in)


def _conv_out_kernel(u_ref, halo_ref, dw_ref, dwb_ref, lng_ref, lnb_ref, wout_ref, bout_ref,
                     h_ref, gate_ref, gpost_ref, o_ref, sh_ref, cv_ref, a_ref, *, seq_tiles):
    tm, d = u_ref.shape
    first = (pl.program_id(0) % seq_tiles) == 0
    sh_ref[0, 0:HALO, :] = jnp.where(first, 0.0, halo_ref[...])
    sh_ref[0, HALO:, :] = u_ref[...]

    def shift_step(i, carry):
        r0 = pl.multiple_of(i * SUBLANES, SUBLANES)
        x = sh_ref[0, pl.ds(r0, 2 * SUBLANES), :]
        for b in range(1, SUBLANES):
            sh_ref[b, pl.ds(r0, SUBLANES), :] = x[b:b + SUBLANES]
        return carry
    lax.fori_loop(0, (tm + HALO) // SUBLANES - 1, shift_step, 0)

    base = HALO - (CONV_WIDTH - 1)
    groups = CONV_ROWS // SUBLANES
    taps_by_shift = {}
    for j in range(CONV_WIDTH):
        a, b = divmod(base + j, SUBLANES)
        taps_by_shift.setdefault(b, []).append((a, j))

    def conv_step(i, carry):
        r0 = pl.multiple_of(i * CONV_ROWS, CONV_ROWS)
        for c in range(d // CONV_LANES):
            lanes = slice(c * CONV_LANES, (c + 1) * CONV_LANES)
            accs = [jnp.broadcast_to(dwb_ref[:, lanes], (SUBLANES, CONV_LANES))] * groups
            for b, taps in taps_by_shift.items():
                ws = {j: dw_ref[j, :, lanes] for _, j in taps}
                tiles = [a for a, _ in taps]
                for k in range(min(tiles), max(tiles) + groups):
                    x = sh_ref[b, pl.ds(r0 + k * SUBLANES, SUBLANES), lanes]
                    for a, j in taps:
                        if 0 <= k - a < groups:
                            accs[k - a] = accs[k - a] + x * ws[j]
            for g in range(groups):
                cv_ref[pl.ds(r0 + g * SUBLANES, SUBLANES), lanes] = accs[g]
        return carry
    lax.fori_loop(0, tm // CONV_ROWS, conv_step, 0)

    lng = lng_ref[...]
    lnb = lnb_ref[...]

    def ln_body(chunks):
        for rows in chunks:
            x = cv_ref[rows, :]
            mu = jnp.mean(x, axis=-1, keepdims=True)
            xc = x - mu
            var = jnp.mean(xc * xc, axis=-1, keepdims=True)
            y = (xc * lax.rsqrt(var + EPS)) * lng + lnb
            a_ref[rows, :] = (y * _sigmoid(y)).astype(BF16)
    _row_loop(tm, ln_body)

    cv_ref[...] = _dot(a_ref[...], wout_ref[...]) + bout_ref[...]
    _residual_gate_norm(h_ref, cv_ref, gate_ref, gpost_ref, o_ref, tm)


def _conv_out(u, dw8, dw_b, ln_g, ln_b, w_out, b_out, h, gate, g_post, seq, tm=256):
    t, d = u.shape
    seq_tiles = seq // tm
    halo_blocks = tm // HALO
    return pl.pallas_call(
        functools.partial(_conv_out_kernel, seq_tiles=seq_tiles),
        grid=(t // tm,),
        in_specs=[
            pl.BlockSpec((tm, d), lambda m: (m, 0)),
            pl.BlockSpec((HALO, d), lambda m: (jnp.maximum(m * halo_blocks - 1, 0), 0)),
            pl.BlockSpec(dw8.shape, lambda m: (0, 0, 0), pipeline_mode=pl.Buffered(1)),
            _row_spec(d), _row_spec(d), _row_spec(d),
            pl.BlockSpec((d, d), lambda m: (0, 0), pipeline_mode=pl.Buffered(1)),
            _row_spec(d),
            pl.BlockSpec((tm, d), lambda m: (m, 0)),
            _vec_spec(d, seq_tiles), _row_spec(d),
        ],
        out_specs=pl.BlockSpec((tm, d), lambda m: (m, 0)),
        out_shape=jax.ShapeDtypeStruct((t, d), F32),
        scratch_shapes=[pltpu.VMEM((SUBLANES, tm + HALO, d), F32), pltpu.VMEM((tm, d), F32),
                        pltpu.VMEM((tm, d), BF16)],
        compiler_params=_cparams("arbitrary"),
        name="conv_out",
    )(u, u, dw8, dw_b, ln_g, ln_b, w_out, b_out, h, gate, g_post)


def _mlp_kernel(h_ref, shift_ref, scale_ref, gate_ref, gpre_ref, gpost_ref, wup_ref, wdown_ref,
                o_ref, hn_ref):
    tm = h_ref.shape[0]
    f = pl.program_id(1)
    last = pl.num_programs(1) - 1

    def hidden():
        up = jnp.maximum(_dot(hn_ref[...], wup_ref[...].astype(BF16)), 0.0)
        return (up * up).astype(BF16)

    @pl.when(f == 0)
    def _():
        _norm_modulate_to(h_ref, gpre_ref, shift_ref, scale_ref, hn_ref, tm, inline=True)
        o_ref[...] = _dot(hidden(), wdown_ref[...].astype(BF16))

    @pl.when((f != 0) & (f != last))
    def _():
        o_ref[...] += _dot(hidden(), wdown_ref[...].astype(BF16))

    @pl.when(f == last)
    def _():
        hid = hidden()
        wdown = wdown_ref[...].astype(BF16)
        for r0 in range(0, tm, MLP_TAIL_ROWS):
            rows = slice(r0, r0 + MLP_TAIL_ROWS)
            o_ref[rows, :] += _dot(hid[rows], wdown)
            _residual_gate_norm(h_ref, o_ref, gate_ref, gpost_ref, o_ref, MLP_TAIL_ROWS, row0=r0, inline=True)


def _mlp(h, shift, scale, gate, g_pre, g_post, w_up, w_down, layer, seq, tm=1024, tf=512):
    t, d = h.shape
    ff = w_up.shape[2]
    seq_tiles = seq // tm
    return pl.pallas_call(
        _mlp_kernel,
        grid=(t // tm, ff // tf),
        in_specs=[
            pl.BlockSpec((tm, d), lambda m, f: (m, 0), pipeline_mode=pl.Buffered(1)),
            _vec_spec(d, seq_tiles), _vec_spec(d, seq_tiles), _vec_spec(d, seq_tiles),
            _row_spec(d), _row_spec(d),
            pl.BlockSpec((None, d, tf), lambda m, f: (layer, 0, f)),
            pl.BlockSpec((None, tf, d), lambda m, f: (layer, f, 0)),
        ],
        out_specs=pl.BlockSpec((tm, d), lambda m, f: (m, 0)),
        out_shape=jax.ShapeDtypeStruct((t, d), F32),
        scratch_shapes=[pltpu.VMEM((tm, d), BF16)],
        compiler_params=_cparams("arbitrary", "arbitrary"),
        name="mlp",
    )(h, shift, scale, gate, g_pre, g_post, w_up, w_down)


def _rope_t(x, cos, sin):
    half = QK_ROPE_DIM // 2
    x1, x2 = x[:half], x[half:]
    return x1 * cos - x2 * sin, x2 * cos + x1 * sin


def _proj_kernel(h_ref, pos_ref, inv_ref, kvshift_ref, kvscale_ref, shift_ref, scale_ref,
                 gkv_ref, gpre_ref, wdkv_ref, gckv_ref, wuk_ref, wuvt_ref, wdq_ref, gcq_ref, wuqt_ref,
                 qt_ref, k_ref, vt_ref, kvn_ref, hn_ref, *, q_scale):
    tm = h_ref.shape[0]
    kv_mul = gkv_ref[...] * (1.0 + kvscale_ref[0])
    kv_add = kvshift_ref[0]
    q_mul = gpre_ref[...] * (1.0 + scale_ref[0])
    q_add = shift_ref[0]

    def norm_body(chunks):
        for rows in chunks:
            x = h_ref[rows, :]
            xn = x * _inv_rms(x)
            kvn_ref[rows, :] = (xn * kv_mul + kv_add).astype(BF16)
            hn_ref[rows, :] = (xn * q_mul + q_add).astype(BF16)
    _row_loop(tm, norm_body)

    ang = inv_ref[...] * pos_ref[...].astype(F32)
    cos = jnp.cos(ang)
    sin = jnp.sin(ang)

    r_kv = gckv_ref.shape[1]
    t1 = _dot(kvn_ref[...], wdkv_ref[...])
    ckv = t1[:, :r_kv]
    ckv = ((ckv * _inv_rms(ckv)) * gckv_ref[...]).astype(BF16)
    kr1, kr2 = _rope_t(t1[:, r_kv:].T[:QK_ROPE_DIM], cos, sin)
    k_rope = jnp.concatenate([kr1, kr2, jnp.zeros((LANES - QK_ROPE_DIM, tm), F32)], axis=0).T
    k_rope = k_rope[:, :QK_ROPE_DIM].astype(BF16)
    k_nope = _dot(ckv, wuk_ref[...]).astype(BF16)
    vt = _dot_nt(wuvt_ref[...], ckv).astype(BF16)

    cq = _dot(hn_ref[...], wdq_ref[...])
    cq = ((cq * _inv_rms(cq)) * gcq_ref[...]).astype(BF16)
    qt = _dot_nt(wuqt_ref[...], cq) * q_scale
    for hd in range(N_HEADS):
        r0 = hd * QK_DIM
        q1, q2 = _rope_t(qt[r0 + QK_NOPE_DIM:r0 + QK_DIM], cos, sin)
        qt_ref[0, hd, :QK_NOPE_DIM, :] = qt[r0:r0 + QK_NOPE_DIM].astype(BF16)
        qt_ref[0, hd, QK_NOPE_DIM:QK_NOPE_DIM + QK_ROPE_DIM // 2, :] = q1.astype(BF16)
        qt_ref[0, hd, QK_NOPE_DIM + QK_ROPE_DIM // 2:, :] = q2.astype(BF16)
        k_ref[0, hd, :, :QK_NOPE_DIM] = k_nope[:, hd * QK_NOPE_DIM:(hd + 1) * QK_NOPE_DIM]
        k_ref[0, hd, :, QK_NOPE_DIM:] = k_rope
        vt_ref[0, hd, 0] = vt[hd * V_HEAD_DIM:(hd + 1) * V_HEAD_DIM, :]


def _proj(h, pos, inv_col, kv_shift, kv_scale, shift, scale, g_kv, g_pre, w_dkvkr, g_ckv, w_uk, w_uvt,
          w_dq, g_cq, w_uqt, q_scale, batch, seq):
    t, d = h.shape
    tm = PROJ_ROWS
    seq_tiles = seq // tm
    r_kv = g_ckv.shape[1]
    r_q = g_cq.shape[1]

    def full(a):
        return pl.BlockSpec(a.shape, lambda m: (0,) * a.ndim, pipeline_mode=pl.Buffered(1))

    return pl.pallas_call(
        functools.partial(_proj_kernel, q_scale=q_scale),
        grid=(t // tm,),
        in_specs=[
            pl.BlockSpec((tm, d), lambda m: (m, 0)),
            pl.BlockSpec((1, tm), lambda m: (0, m)),
            full(inv_col),
            _vec_spec(d, seq_tiles), _vec_spec(d, seq_tiles), _vec_spec(d, seq_tiles), _vec_spec(d, seq_tiles),
            _row_spec(d), _row_spec(d),
            full(w_dkvkr), _row_spec(r_kv), full(w_uk), full(w_uvt), full(w_dq), _row_spec(r_q), full(w_uqt),
        ],
        out_specs=[
            pl.BlockSpec((1, N_HEADS, QK_DIM, tm), lambda m: (m // seq_tiles, 0, 0, m % seq_tiles)),
            pl.BlockSpec((1, N_HEADS, tm, QK_DIM), lambda m: (m // seq_tiles, 0, m % seq_tiles, 0)),
            pl.BlockSpec((1, N_HEADS, 1, V_HEAD_DIM, tm), lambda m: (m // seq_tiles, 0, m % seq_tiles, 0, 0)),
        ],
        out_shape=[
            jax.ShapeDtypeStruct((batch, N_HEADS, QK_DIM, seq), BF16),
            jax.ShapeDtypeStruct((batch, N_HEADS, seq, QK_DIM), BF16),
            jax.ShapeDtypeStruct((batch, N_HEADS, seq_tiles, V_HEAD_DIM, tm), BF16),
        ],
        scratch_shapes=[pltpu.VMEM((tm, d), BF16), pltpu.VMEM((tm, d), BF16)],
        compiler_params=_cparams("arbitrary"),
        name="proj",
    )(h, pos, inv_col, kv_shift, kv_scale, shift, scale, g_kv, g_pre, w_dkvkr, g_ckv, w_uk, w_uvt, w_dq, g_cq, w_uqt)


def _attn_kernel(qt_ref, k_ref, vt_ref, o_ref, sa_ref, sb_ref, acc_ref, st_ref, *, blk):
    qi = pl.program_id(2)
    heads = range(qt_ref.shape[1])
    half = vt_ref.shape[-1]
    assert blk == 2 * half

    def scores_to(s_ref, hd, tile):
        k = k_ref[0, hd, pl.ds(pl.multiple_of(tile * half, half), half), :]
        s = _dot(k, qt_ref[0, hd])
        s_ref[hd] = s
        return jnp.max(s, axis=0, keepdims=True)

    def update(s, mx, hd, tile, m, l):
        m_new = jnp.maximum(m, mx)
        alpha = jnp.exp2(m - m_new)
        p = jnp.exp2(s - m_new)
        l = alpha * l + jnp.sum(p, axis=0, keepdims=True)
        acc_ref[hd] = alpha * acc_ref[hd] + _dot(vt_ref[0, hd, tile], p.astype(BF16))
        return m_new, l

    def body(j, carry):
        out = []
        for hd, (mxa, m, l) in zip(heads, carry):
            mxb = scores_to(sb_ref, hd, 2 * j + 1)
            m, l = update(sa_ref[hd], mxa, hd, 2 * j, m, l)
            mxa = scores_to(sa_ref, hd, 2 * j + 2)
            m, l = update(sb_ref[hd], mxb, hd, 2 * j + 1, m, l)
            out.append((mxa, m, l))
        return tuple(out)

    acc_ref[...] = jnp.zeros_like(acc_ref)
    init = tuple((scores_to(sa_ref, hd, 0), jnp.full((1, blk), NEG, F32), jnp.zeros((1, blk), F32))
                 for hd in heads)
    carry = lax.fori_loop(0, qi, body, init)

    key = lax.broadcasted_iota(jnp.int32, (half, blk), 0)
    qry = lax.broadcasted_iota(jnp.int32, (half, blk), 1)
    key_sq = lax.broadcasted_iota(jnp.int32, (half, half), 0)
    qry_sq = lax.broadcasted_iota(jnp.int32, (half, half), 1)
    for hd, (_, m, l) in zip(heads, carry):
        kb = k_ref[0, hd, pl.ds(pl.multiple_of((2 * qi + 1) * half, half), half), :]
        sb = jnp.where(key_sq <= qry_sq, _dot(kb, qt_ref[0, hd, :, half:]), NEG)
        sa = jnp.where(key <= qry, sa_ref[hd], NEG)
        m, l = update(sa, jnp.max(sa, axis=0, keepdims=True), hd, 2 * qi, m, l)
        st_ref[0:1, :] = m
        st_ref[1:2, :] = l
        m_new = jnp.maximum(st_ref[0:1, half:], jnp.max(sb, axis=0, keepdims=True))
        alpha = jnp.exp2(st_ref[0:1, half:] - m_new)
        p = jnp.exp2(sb - m_new)
        st_ref[1:2, half:] = alpha * st_ref[1:2, half:] + jnp.sum(p, axis=0, keepdims=True)
        acc_ref[hd, :, half:] = alpha * acc_ref[hd, :, half:] + _dot(vt_ref[0, hd, 2 * qi + 1], p.astype(BF16))
        o_ref[0, hd] = (acc_ref[hd] / st_ref[1:2, :]).T.astype(BF16)


def _attention(qt, k, vt, blk=ATTN_BLOCK, nh=ATTN_HEADS):
    b, n_heads, s, _ = k.shape
    half = vt.shape[-1]
    return pl.pallas_call(
        functools.partial(_attn_kernel, blk=blk),
        grid=(b, n_heads // nh, s // blk),
        in_specs=[
            pl.BlockSpec((1, nh, QK_DIM, blk), lambda bi, hi, qi: (bi, hi, 0, qi)),
            pl.BlockSpec((1, nh, s, QK_DIM), lambda bi, hi, qi: (bi, hi, 0, 0)),
            pl.BlockSpec((1, nh) + vt.shape[2:], lambda bi, hi, qi: (bi, hi, 0, 0, 0)),
        ],
        out_specs=pl.BlockSpec((1, nh, blk, V_HEAD_DIM), lambda bi, hi, qi: (bi, hi, qi, 0)),
        out_shape=jax.ShapeDtypeStruct((b, n_heads, s, V_HEAD_DIM), BF16),
        scratch_shapes=[pltpu.VMEM((nh, half, blk), F32), pltpu.VMEM((nh, half, blk), F32),
                        pltpu.VMEM((nh, V_HEAD_DIM, blk), F32), pltpu.VMEM((SUBLANES, blk), F32)],
        compiler_params=_cparams("arbitrary", "arbitrary", "arbitrary"),
        name="attn",
    )(qt, k, vt)


def _attn_out_kernel(a_ref, wo_ref, h_ref, gate_ref, gpost_ref, o_ref, cat_ref):
    tm = h_ref.shape[0]
    for hd in range(N_HEADS):
        cat_ref[:, hd * V_HEAD_DIM:(hd + 1) * V_HEAD_DIM] = a_ref[0, hd]
    o_ref[...] = _dot(cat_ref[...], wo_ref[...])
    _residual_gate_norm(h_ref, o_ref, gate_ref, gpost_ref, o_ref, tm)


def _attn_out(a, w_o, h, gate, g_post, seq, tm=256):
    t, d = h.shape
    seq_tiles = seq // tm
    return pl.pallas_call(
        _attn_out_kernel,
        grid=(t // tm,),
        in_specs=[
            pl.BlockSpec((1, N_HEADS, tm, V_HEAD_DIM), lambda m: (m // seq_tiles, 0, m % seq_tiles, 0)),
            pl.BlockSpec(w_o.shape, lambda m: (0, 0), pipeline_mode=pl.Buffered(1)),
            pl.BlockSpec((tm, d), lambda m: (m, 0)),
            _vec_spec(d, seq_tiles), _row_spec(d),
        ],
        out_specs=pl.BlockSpec((tm, d), lambda m: (m, 0)),
        out_shape=jax.ShapeDtypeStruct((t, d), F32),
        scratch_shapes=[pltpu.VMEM((tm, N_HEADS * V_HEAD_DIM), BF16)],
        compiler_params=_cparams("arbitrary"),
        name="attn_out",
    )(a, w_o, h, gate, g_post)


def kernel(x, c, positions, w_ada_mix, b_ada_mix, w_ada_mlp, b_ada_mlp, g_pre_mix, g_post_mix, g_pre_mlp,
           g_post_mlp, conv_w_in, conv_b_in, conv_dw, conv_dw_b, conv_ln_g, conv_ln_b, conv_w_out, conv_b_out,
           w_ada_kv, b_ada_kv, g_kv, w_dkv, g_ckv, w_kr, w_uk, w_uv, w_dq, g_cq, w_uq, w_o, mlp_w_up,
           mlp_w_down):
    batch, seq, d = x.shape
    depth = w_ada_mix.shape[0]
    n_conv = conv_w_in.shape[0]
    t = batch * seq
    if depth - n_conv != 1:
        raise NotImplementedError("exactly one MLA layer reads the shared K/V in this trunk")

    c_pad = jnp.pad(c, ((0, -batch % SUBLANES), (0, 0)))

    def split(m, n):
        return [m[:batch, None, i * d:(i + 1) * d] for i in range(n)]

    ada_mix = _ada(c_pad, w_ada_mix, b_ada_mix)
    ada_mlp = _ada(c_pad, w_ada_mlp, b_ada_mlp)
    kv_shift, kv_scale = split(_ada(c_pad, w_ada_kv[None], b_ada_kv[None])[0], 2)

    def row(v):
        return v.reshape(1, -1)

    inv = 1.0 / (ROPE_THETA ** (jnp.arange(0, QK_ROPE_DIM, 2, dtype=F32) / QK_ROPE_DIM))
    inv_col = inv.reshape(-1, 1)
    pos = positions.reshape(1, t)
    q_scale = QK_DIM ** -0.5 * math.log2(math.e)

    h = x.reshape(t, d)
    for l in range(depth):
        shift, scale, gate = split(ada_mix[l], 3)
        if l < n_conv:
            u = _conv_in(h, shift, scale, row(g_pre_mix[l]), conv_w_in, row(conv_b_in[l]), l, seq)
            dw8 = jnp.broadcast_to(conv_dw[l][:, None, :], (CONV_WIDTH, SUBLANES, d))
            h = _conv_out(u, dw8, row(conv_dw_b[l]), row(conv_ln_g[l]), row(conv_ln_b[l]),
                          conv_w_out[l].astype(BF16), row(conv_b_out[l]), h, gate, row(g_post_mix[l]), seq)
        else:
            j = l - n_conv
            w_dkvkr = jnp.concatenate(
                [w_dkv, w_kr, jnp.zeros((d, LANES - QK_ROPE_DIM), F32)], axis=1).astype(BF16)
            qt, k, vt = _proj(h, pos, inv_col, kv_shift, kv_scale, shift, scale, row(g_kv), row(g_pre_mix[l]),
                              w_dkvkr, row(g_ckv), w_uk.astype(BF16), w_uv.T.astype(BF16),
                              w_dq[j].astype(BF16), row(g_cq[j]), w_uq[j].T.astype(BF16), q_scale, batch, seq)
            a = _attention(qt, k, vt)
            h = _attn_out(a, w_o[j].astype(BF16), h, gate, row(g_post_mix[l]), seq)
        shift, scale, gate = split(ada_mlp[l], 3)
        h = _mlp(h, shift, scale, gate, row(g_pre_mlp[l]), row(g_post_mlp[l]), mlp_w_up, mlp_w_down, l, seq)
    return h.reshape(batch, seq, d)
```

```python
import functools
import math

import jax
import jax.numpy as jnp
from jax import lax
from jax.experimental import pallas as pl
from jax.experimental.pallas import tpu as pltpu

F32 = jnp.float32
BF16 = jnp.bfloat16

EPS = 1e-6
NEG = -1e30
ROPE_THETA = 10000.0

N_HEADS = 16
QK_NOPE_DIM = 128
QK_ROPE_DIM = 64
QK_DIM = QK_NOPE_DIM + QK_ROPE_DIM
V_HEAD_DIM = 128
CONV_WIDTH = 31

LANES = 128
SUBLANES = 8
VMEM_BYTES_V7X = 64 * 1024 * 1024
VMEM_LIMIT = VMEM_BYTES_V7X - 8 * 1024 * 1024

ROW_CHUNK = 16
ROW_GROUP = 16
CONV_ROWS = 32
CONV_LANES = 256
HALO = 32
MLP_TAIL_ROWS = 256
PROJ_ROWS = 512
VT_TILE = 256
ATTN_BLOCK = 512
ATTN_HEADS = 4


def _cparams(*sem, flags=None):
    return pltpu.CompilerParams(dimension_semantics=sem, vmem_limit_bytes=VMEM_LIMIT, flags=flags)


def _dot(a, b):
    return jnp.dot(a, b, preferred_element_type=F32)


def _dot_nt(a, b):
    return lax.dot_general(a, b, (((1,), (1,)), ((), ())), preferred_element_type=F32)


def _sigmoid(x):
    return 1.0 / (1.0 + jnp.exp(-x))


def _row_loop(n_rows, body, row0=0, inline=False):
    span = ROW_CHUNK * ROW_GROUP
    if inline:
        for r0 in range(row0, row0 + n_rows, span):
            body([pl.ds(r0 + k * ROW_CHUNK, ROW_CHUNK) for k in range(ROW_GROUP)])
        return

    def step(i, carry):
        r0 = pl.multiple_of(row0 + i * span, span)
        body([pl.ds(r0 + k * ROW_CHUNK, ROW_CHUNK) for k in range(ROW_GROUP)])
        return carry
    lax.fori_loop(0, n_rows // span, step, 0)


def _inv_rms(x):
    return lax.rsqrt(jnp.mean(x * x, axis=-1, keepdims=True) + EPS)


def _ada_kernel(c_ref, w_ref, b_ref, o_ref):
    @pl.when(pl.program_id(1) == 0)
    def _():
        o_ref[0] = jnp.broadcast_to(b_ref[0], o_ref.shape[1:])

    c = c_ref[...]
    o_ref[0] += _dot((c * _sigmoid(c)).astype(BF16), w_ref[0].astype(BF16))


def _ada(c_pad, w, b, tk=256):
    nl, d, n = w.shape
    rows = c_pad.shape[0]
    return pl.pallas_call(
        _ada_kernel,
        grid=(nl, d // tk),
        in_specs=[
            pl.BlockSpec((rows, tk), lambda l, k: (0, k)),
            pl.BlockSpec((1, tk, n), lambda l, k: (l, k, 0)),
            pl.BlockSpec((1, 1, n), lambda l, k: (l, 0, 0)),
        ],
        out_specs=pl.BlockSpec((1, rows, n), lambda l, k: (l, 0, 0)),
        out_shape=jax.ShapeDtypeStruct((nl, rows, n), F32),
        compiler_params=_cparams("arbitrary", "arbitrary"),
        name="ada",
    )(c_pad, w, b.reshape(nl, 1, n))


def _norm_modulate_to(h_ref, g_ref, shift_ref, scale_ref, out_ref, n_rows, **loop_kw):
    mul = g_ref[...] * (1.0 + scale_ref[0])
    add = shift_ref[0]

    def body(chunks):
        for rows in chunks:
            x = h_ref[rows, :]
            out_ref[rows, :] = ((x * _inv_rms(x)) * mul + add).astype(BF16)
    _row_loop(n_rows, body, **loop_kw)


def _residual_gate_norm(h_ref, y_ref, gate_ref, g_ref, o_ref, n_rows, **loop_kw):
    mul = gate_ref[0] * g_ref[...]

    def body(chunks):
        ys = [y_ref[rows, :] for rows in chunks]
        scaled = [y * _inv_rms(y) for y in ys]
        for rows, s in zip(chunks, scaled):
            o_ref[rows, :] = h_ref[rows, :] + s * mul
    _row_loop(n_rows, body, **loop_kw)


def _vec_spec(d, seq_tiles):
    return pl.BlockSpec((1, 1, d), lambda m, *_: (m // seq_tiles, 0, 0))


def _row_spec(d):
    return pl.BlockSpec((1, d), lambda *_: (0, 0))


def _conv_in_kernel(h_ref, shift_ref, scale_ref, g_ref, wa_ref, wg_ref, ba_ref, bg_ref, u_ref, hn_ref):
    tm = h_ref.shape[0]

    def glu():
        hn = hn_ref[...]
        a = _dot(hn, wa_ref[...].astype(BF16)) + ba_ref[...]
        g = _dot(hn, wg_ref[...].astype(BF16)) + bg_ref[...]
        u_ref[...] = a * _sigmoid(g)

    @pl.when(pl.program_id(1) == 0)
    def _():
        _norm_modulate_to(h_ref, g_ref, shift_ref, scale_ref, hn_ref, tm, inline=True)
        glu()

    @pl.when(pl.program_id(1) != 0)
    def _():
        glu()


def _conv_in(h, shift, scale, g_pre, w_in, b_in, layer, seq, tm=1024, tn=512):
    t, d = h.shape
    nt = d // tn
    return pl.pallas_call(
        _conv_in_kernel,
        grid=(t // tm, nt),
        in_specs=[
            pl.BlockSpec((tm, d), lambda m, n: (m, 0)),
            _vec_spec(d, seq // tm), _vec_spec(d, seq // tm), _row_spec(d),
            pl.BlockSpec((None, d, tn), lambda m, n: (layer, 0, n)),
            pl.BlockSpec((None, d, tn), lambda m, n: (layer, 0, n + nt)),
            pl.BlockSpec((1, tn), lambda m, n: (0, n)),
            pl.BlockSpec((1, tn), lambda m, n: (0, n + nt)),
        ],
        out_specs=pl.BlockSpec((tm, tn), lambda m, n: (m, n)),
        out_shape=jax.ShapeDtypeStruct((t, d), F32),
        scratch_shapes=[pltpu.VMEM((tm, d), BF16)],
        compiler_params=_cparams("arbitrary", "arbitrary"),
        name="conv_in",
    )(h, shift, scale, g_pre, w_in, w_in, b_in, b_in)


def _conv_out_kernel(u_ref, halo_ref, dw_ref, dwb_ref, lng_ref, lnb_ref, wout_ref, bout_ref,
                     h_ref, gate_ref, gpost_ref, o_ref, sh_ref, cv_ref, a_ref, *, seq_tiles):
    tm, d = u_ref.shape
    first = (pl.program_id(0) % seq_tiles) == 0
    sh_ref[0, 0:HALO, :] = jnp.where(first, 0.0, halo_ref[...])
    sh_ref[0, HALO:, :] = u_ref[...]

    def shift_step(i, carry):
        r0 = pl.multiple_of(i * SUBLANES, SUBLANES)
        x = sh_ref[0, pl.ds(r0, 2 * SUBLANES), :]
        for b in range(1, SUBLANES):
            sh_ref[b, pl.ds(r0, SUBLANES), :] = x[b:b + SUBLANES]
        return carry
    lax.fori_loop(0, (tm + HALO) // SUBLANES - 1, shift_step, 0)

    base = HALO - (CONV_WIDTH - 1)
    groups = CONV_ROWS // SUBLANES
    taps_by_shift = {}
    for j in range(CONV_WIDTH):
        a, b = divmod(base + j, SUBLANES)
        taps_by_shift.setdefault(b, []).append((a, j))

    def conv_step(i, carry):
        r0 = pl.multiple_of(i * CONV_ROWS, CONV_ROWS)
        for c in range(d // CONV_LANES):
            lanes = slice(c * CONV_LANES, (c + 1) * CONV_LANES)
            accs = [jnp.broadcast_to(dwb_ref[:, lanes], (SUBLANES, CONV_LANES))] * groups
            for b, taps in taps_by_shift.items():
                ws = {j: dw_ref[j, :, lanes] for _, j in taps}
                tiles = [a for a, _ in taps]
                for k in range(min(tiles), max(tiles) + groups):
                    x = sh_ref[b, pl.ds(r0 + k * SUBLANES, SUBLANES), lanes]
                    for a, j in taps:
                        if 0 <= k - a < groups:
                            accs[k - a] = accs[k - a] + x * ws[j]
            for g in range(groups):
                cv_ref[pl.ds(r0 + g * SUBLANES, SUBLANES), lanes] = accs[g]
        return carry
    lax.fori_loop(0, tm // CONV_ROWS, conv_step, 0)

    lng = lng_ref[...]
    lnb = lnb_ref[...]

    def ln_body(chunks):
        for rows in chunks:
            x = cv_ref[rows, :]
            mu = jnp.mean(x, axis=-1, keepdims=True)
            xc = x - mu
            var = jnp.mean(xc * xc, axis=-1, keepdims=True)
            y = (xc * lax.rsqrt(var + EPS)) * lng + lnb
            a_ref[rows, :] = (y * _sigmoid(y)).astype(BF16)
    _row_loop(tm, ln_body, inline=True)

    cv_ref[...] = _dot(a_ref[...], wout_ref[...]) + bout_ref[...]
    _residual_gate_norm(h_ref, cv_ref, gate_ref, gpost_ref, o_ref, tm, inline=True)


def _conv_out(u, dw8, dw_b, ln_g, ln_b, w_out, b_out, h, gate, g_post, seq, tm=256):
    t, d = u.shape
    seq_tiles = seq // tm
    halo_blocks = tm // HALO
    return pl.pallas_call(
        functools.partial(_conv_out_kernel, seq_tiles=seq_tiles),
        grid=(t // tm,),
        in_specs=[
            pl.BlockSpec((tm, d), lambda m: (m, 0)),
            pl.BlockSpec((HALO, d), lambda m: (jnp.maximum(m * halo_blocks - 1, 0), 0)),
            pl.BlockSpec(dw8.shape, lambda m: (0, 0, 0), pipeline_mode=pl.Buffered(1)),
            _row_spec(d), _row_spec(d), _row_spec(d),
            pl.BlockSpec((d, d), lambda m: (0, 0), pipeline_mode=pl.Buffered(1)),
            _row_spec(d),
            pl.BlockSpec((tm, d), lambda m: (m, 0)),
            _vec_spec(d, seq_tiles), _row_spec(d),
        ],
        out_specs=pl.BlockSpec((tm, d), lambda m: (m, 0)),
        out_shape=jax.ShapeDtypeStruct((t, d), F32),
        scratch_shapes=[pltpu.VMEM((SUBLANES, tm + HALO, d), F32), pltpu.VMEM((tm, d), F32),
                        pltpu.VMEM((tm, d), BF16)],
        compiler_params=_cparams("arbitrary"),
        name="conv_out",
    )(u, u, dw8, dw_b, ln_g, ln_b, w_out, b_out, h, gate, g_post)


def _mlp_kernel(h_ref, shift_ref, scale_ref, gate_ref, gpre_ref, gpost_ref, wup_ref, wdown_ref,
                o_ref, hn_ref):
    tm = h_ref.shape[0]
    f = pl.program_id(1)
    last = pl.num_programs(1) - 1

    def hidden():
        up = jnp.maximum(_dot(hn_ref[...], wup_ref[...].astype(BF16)), 0.0)
        return (up * up).astype(BF16)

    @pl.when(f == 0)
    def _():
        _norm_modulate_to(h_ref, gpre_ref, shift_ref, scale_ref, hn_ref, tm, inline=True)
        o_ref[...] = _dot(hidden(), wdown_ref[...].astype(BF16))

    @pl.when((f != 0) & (f != last))
    def _():
        o_ref[...] += _dot(hidden(), wdown_ref[...].astype(BF16))

    @pl.when(f == last)
    def _():
        hid = hidden()
        wdown = wdown_ref[...].astype(BF16)
        for r0 in range(0, tm, MLP_TAIL_ROWS):
            rows = slice(r0, r0 + MLP_TAIL_ROWS)
            o_ref[rows, :] += _dot(hid[rows], wdown)
            _residual_gate_norm(h_ref, o_ref, gate_ref, gpost_ref, o_ref, MLP_TAIL_ROWS, row0=r0, inline=True)


def _mlp(h, shift, scale, gate, g_pre, g_post, w_up, w_down, layer, seq, tm=1024, tf=512):
    t, d = h.shape
    ff = w_up.shape[2]
    seq_tiles = seq // tm
    return pl.pallas_call(
        _mlp_kernel,
        grid=(t // tm, ff // tf),
        in_specs=[
            pl.BlockSpec((tm, d), lambda m, f: (m, 0), pipeline_mode=pl.Buffered(1)),
            _vec_spec(d, seq_tiles), _vec_spec(d, seq_tiles), _vec_spec(d, seq_tiles),
            _row_spec(d), _row_spec(d),
            pl.BlockSpec((None, d, tf), lambda m, f: (layer, 0, f)),
            pl.BlockSpec((None, tf, d), lambda m, f: (layer, f, 0)),
        ],
        out_specs=pl.BlockSpec((tm, d), lambda m, f: (m, 0)),
        out_shape=jax.ShapeDtypeStruct((t, d), F32),
        scratch_shapes=[pltpu.VMEM((tm, d), BF16)],
        compiler_params=_cparams("arbitrary", "arbitrary"),
        name="mlp",
    )(h, shift, scale, gate, g_pre, g_post, w_up, w_down)


def _rope_t(x, cos, sin):
    half = QK_ROPE_DIM // 2
    x1, x2 = x[:half], x[half:]
    return x1 * cos - x2 * sin, x2 * cos + x1 * sin


def _proj_kernel(h_ref, pos_ref, inv_ref, kvshift_ref, kvscale_ref, shift_ref, scale_ref,
                 gkv_ref, gpre_ref, wdkv_ref, gckv_ref, wuk_ref, wuvt_ref, wdq_ref, gcq_ref, wuqt_ref,
                 qt_ref, k_ref, vt_ref, kvn_ref, hn_ref, *, q_scale):
    tm = h_ref.shape[0]
    kv_mul = gkv_ref[...] * (1.0 + kvscale_ref[0])
    kv_add = kvshift_ref[0]
    q_mul = gpre_ref[...] * (1.0 + scale_ref[0])
    q_add = shift_ref[0]

    def norm_body(chunks):
        for rows in chunks:
            x = h_ref[rows, :]
            xn = x * _inv_rms(x)
            kvn_ref[rows, :] = (xn * kv_mul + kv_add).astype(BF16)
            hn_ref[rows, :] = (xn * q_mul + q_add).astype(BF16)
    _row_loop(tm, norm_body, inline=True)

    ang = inv_ref[...] * pos_ref[...].astype(F32)
    cos = jnp.cos(ang)
    sin = jnp.sin(ang)

    r_kv = gckv_ref.shape[1]
    t1 = _dot(kvn_ref[...], wdkv_ref[...])
    ckv = t1[:, :r_kv]
    ckv = ((ckv * _inv_rms(ckv)) * gckv_ref[...]).astype(BF16)
    kr1, kr2 = _rope_t(t1[:, r_kv:].T[:QK_ROPE_DIM], cos, sin)
    k_rope = jnp.concatenate([kr1, kr2, jnp.zeros((LANES - QK_ROPE_DIM, tm), F32)], axis=0).T
    k_rope = k_rope[:, :QK_ROPE_DIM].astype(BF16)
    k_nope = _dot(ckv, wuk_ref[...]).astype(BF16)
    vt = _dot_nt(wuvt_ref[...], ckv).astype(BF16)

    cq = _dot(hn_ref[...], wdq_ref[...])
    cq = ((cq * _inv_rms(cq)) * gcq_ref[...]).astype(BF16)
    qt = _dot_nt(wuqt_ref[...], cq) * q_scale
    for hd in range(N_HEADS):
        r0 = hd * QK_DIM
        q1, q2 = _rope_t(qt[r0 + QK_NOPE_DIM:r0 + QK_DIM], cos, sin)
        qt_ref[0, hd, :QK_NOPE_DIM, :] = qt[r0:r0 + QK_NOPE_DIM].astype(BF16)
        qt_ref[0, hd, QK_NOPE_DIM:QK_NOPE_DIM + QK_ROPE_DIM // 2, :] = q1.astype(BF16)
        qt_ref[0, hd, QK_NOPE_DIM + QK_ROPE_DIM // 2:, :] = q2.astype(BF16)
        k_ref[0, hd, :, :QK_NOPE_DIM] = k_nope[:, hd * QK_NOPE_DIM:(hd + 1) * QK_NOPE_DIM]
        k_ref[0, hd, :, QK_NOPE_DIM:] = k_rope
        for i in range(vt_ref.shape[2]):
            vt_ref[0, hd, i] = vt[hd * V_HEAD_DIM:(hd + 1) * V_HEAD_DIM, i * VT_TILE:(i + 1) * VT_TILE]


def _proj(h, pos, inv_col, kv_shift, kv_scale, shift, scale, g_kv, g_pre, w_dkvkr, g_ckv, w_uk, w_uvt,
          w_dq, g_cq, w_uqt, q_scale, batch, seq):
    t, d = h.shape
    tm = PROJ_ROWS
    seq_tiles = seq // tm
    r_kv = g_ckv.shape[1]
    r_q = g_cq.shape[1]

    def full(a):
        return pl.BlockSpec(a.shape, lambda m: (0,) * a.ndim, pipeline_mode=pl.Buffered(1))

    return pl.pallas_call(
        functools.partial(_proj_kernel, q_scale=q_scale),
        grid=(t // tm,),
        in_specs=[
            pl.BlockSpec((tm, d), lambda m: (m, 0)),
            pl.BlockSpec((1, tm), lambda m: (0, m)),
            full(inv_col),
            _vec_spec(d, seq_tiles), _vec_spec(d, seq_tiles), _vec_spec(d, seq_tiles), _vec_spec(d, seq_tiles),
            _row_spec(d), _row_spec(d),
            full(w_dkvkr), _row_spec(r_kv), full(w_uk), full(w_uvt), full(w_dq), _row_spec(r_q), full(w_uqt),
        ],
        out_specs=[
            pl.BlockSpec((1, N_HEADS, QK_DIM, tm), lambda m: (m // seq_tiles, 0, 0, m % seq_tiles)),
            pl.BlockSpec((1, N_HEADS, tm, QK_DIM), lambda m: (m // seq_tiles, 0, m % seq_tiles, 0)),
            pl.BlockSpec((1, N_HEADS, tm // VT_TILE, V_HEAD_DIM, VT_TILE),
                         lambda m: (m // seq_tiles, 0, m % seq_tiles, 0, 0)),
        ],
        out_shape=[
            jax.ShapeDtypeStruct((batch, N_HEADS, QK_DIM, seq), BF16),
            jax.ShapeDtypeStruct((batch, N_HEADS, seq, QK_DIM), BF16),
            jax.ShapeDtypeStruct((batch, N_HEADS, seq // VT_TILE, V_HEAD_DIM, VT_TILE), BF16),
        ],
        scratch_shapes=[pltpu.VMEM((tm, d), BF16), pltpu.VMEM((tm, d), BF16)],
        compiler_params=_cparams("arbitrary"),
        name="proj",
    )(h, pos, inv_col, kv_shift, kv_scale, shift, scale, g_kv, g_pre, w_dkvkr, g_ckv, w_uk, w_uvt, w_dq, g_cq, w_uqt)


def _attn_kernel(qt_ref, k_ref, vt_ref, o_ref, sa_ref, sb_ref, acc_ref, st_ref, *, blk):
    qi = pl.program_id(2)
    heads = range(qt_ref.shape[1])
    half = vt_ref.shape[-1]
    assert blk == 2 * half

    def scores_to(s_ref, hd, tile):
        k = k_ref[0, hd, pl.ds(pl.multiple_of(tile * half, half), half), :]
        s = _dot(k, qt_ref[0, hd])
        s_ref[hd] = s
        return jnp.max(s, axis=0, keepdims=True)

    def update(s, mx, hd, tile, m, l):
        m_new = jnp.maximum(m, mx)
        alpha = jnp.exp2(m - m_new)
        p = jnp.exp2(s - m_new)
        l = alpha * l + jnp.sum(p, axis=0, keepdims=True)
        acc_ref[hd] = alpha * acc_ref[hd] + _dot(vt_ref[0, hd, tile], p.astype(BF16))
        return m_new, l

    def body(j, carry):
        out = []
        for hd, (mxa, m, l) in zip(heads, carry):
            mxb = scores_to(sb_ref, hd, 2 * j + 1)
            m, l = update(sa_ref[hd], mxa, hd, 2 * j, m, l)
            mxa = scores_to(sa_ref, hd, 2 * j + 2)
            m, l = update(sb_ref[hd], mxb, hd, 2 * j + 1, m, l)
            out.append((mxa, m, l))
        return tuple(out)

    acc_ref[...] = jnp.zeros_like(acc_ref)
    init = tuple((scores_to(sa_ref, hd, 0), jnp.full((1, blk), NEG, F32), jnp.zeros((1, blk), F32))
                 for hd in heads)
    carry = lax.fori_loop(0, qi, body, init)

    key = lax.broadcasted_iota(jnp.int32, (half, blk), 0)
    qry = lax.broadcasted_iota(jnp.int32, (half, blk), 1)
    key_sq = lax.broadcasted_iota(jnp.int32, (half, half), 0)
    qry_sq = lax.broadcasted_iota(jnp.int32, (half, half), 1)
    for hd, (_, m, l) in zip(heads, carry):
        kb = k_ref[0, hd, pl.ds(pl.multiple_of((2 * qi + 1) * half, half), half), :]
        sb = jnp.where(key_sq <= qry_sq, _dot(kb, qt_ref[0, hd, :, half:]), NEG)
        sa = jnp.where(key <= qry, sa_ref[hd], NEG)
        m, l = update(sa, jnp.max(sa, axis=0, keepdims=True), hd, 2 * qi, m, l)
        st_ref[0:1, :] = m
        st_ref[1:2, :] = l
        m_new = jnp.maximum(st_ref[0:1, half:], jnp.max(sb, axis=0, keepdims=True))
        alpha = jnp.exp2(st_ref[0:1, half:] - m_new)
        p = jnp.exp2(sb - m_new)
        st_ref[1:2, half:] = alpha * st_ref[1:2, half:] + jnp.sum(p, axis=0, keepdims=True)
        acc_ref[hd, :, half:] = alpha * acc_ref[hd, :, half:] + _dot(vt_ref[0, hd, 2 * qi + 1], p.astype(BF16))
        o_ref[0, hd] = (acc_ref[hd] / st_ref[1:2, :]).T.astype(BF16)


def _attention(qt, k, vt, blk=ATTN_BLOCK, nh=ATTN_HEADS):
    b, n_heads, s, _ = k.shape
    half = vt.shape[-1]
    return pl.pallas_call(
        functools.partial(_attn_kernel, blk=blk),
        grid=(b, n_heads // nh, s // blk),
        in_specs=[
            pl.BlockSpec((1, nh, QK_DIM, blk), lambda bi, hi, qi: (bi, hi, 0, qi)),
            pl.BlockSpec((1, nh, s, QK_DIM), lambda bi, hi, qi: (bi, hi, 0, 0)),
            pl.BlockSpec((1, nh) + vt.shape[2:], lambda bi, hi, qi: (bi, hi, 0, 0, 0)),
        ],
        out_specs=pl.BlockSpec((1, nh, blk, V_HEAD_DIM), lambda bi, hi, qi: (bi, hi, qi, 0)),
        out_shape=jax.ShapeDtypeStruct((b, n_heads, s, V_HEAD_DIM), BF16),
        scratch_shapes=[pltpu.VMEM((nh, half, blk), F32), pltpu.VMEM((nh, half, blk), F32),
                        pltpu.VMEM((nh, V_HEAD_DIM, blk), F32), pltpu.VMEM((SUBLANES, blk), F32)],
        compiler_params=_cparams("arbitrary", "arbitrary", "arbitrary"),
        name="attn",
    )(qt, k, vt)


def _attn_out_kernel(a_ref, wo_ref, h_ref, gate_ref, gpost_ref, o_ref, cat_ref, wbf_ref):
    tm = h_ref.shape[0]

    @pl.when(pl.program_id(0) == 0)
    def _():
        wbf_ref[...] = wo_ref[...].astype(BF16)

    for hd in range(N_HEADS):
        cat_ref[:, hd * V_HEAD_DIM:(hd + 1) * V_HEAD_DIM] = a_ref[0, hd]
    o_ref[...] = _dot(cat_ref[...], wbf_ref[...])
    _residual_gate_norm(h_ref, o_ref, gate_ref, gpost_ref, o_ref, tm, inline=True)


def _attn_out(a, w_o, h, gate, g_post, seq, tm=256):
    t, d = h.shape
    seq_tiles = seq // tm
    return pl.pallas_call(
        _attn_out_kernel,
        grid=(t // tm,),
        in_specs=[
            pl.BlockSpec((1, N_HEADS, tm, V_HEAD_DIM), lambda m: (m // seq_tiles, 0, m % seq_tiles, 0)),
            pl.BlockSpec(w_o.shape, lambda m: (0, 0), pipeline_mode=pl.Buffered(1)),
            pl.BlockSpec((tm, d), lambda m: (m, 0)),
            _vec_spec(d, seq_tiles), _row_spec(d),
        ],
        out_specs=pl.BlockSpec((tm, d), lambda m: (m, 0)),
        out_shape=jax.ShapeDtypeStruct((t, d), F32),
        scratch_shapes=[pltpu.VMEM((tm, N_HEADS * V_HEAD_DIM), BF16), pltpu.VMEM(w_o.shape, BF16)],
        compiler_params=_cparams("arbitrary"),
        name="attn_out",
    )(a, w_o, h, gate, g_post)


def kernel(x, c, positions, w_ada_mix, b_ada_mix, w_ada_mlp, b_ada_mlp, g_pre_mix, g_post_mix, g_pre_mlp,
           g_post_mlp, conv_w_in, conv_b_in, conv_dw, conv_dw_b, conv_ln_g, conv_ln_b, conv_w_out, conv_b_out,
           w_ada_kv, b_ada_kv, g_kv, w_dkv, g_ckv, w_kr, w_uk, w_uv, w_dq, g_cq, w_uq, w_o, mlp_w_up,
           mlp_w_down):
    batch, seq, d = x.shape
    depth = w_ada_mix.shape[0]
    n_conv = conv_w_in.shape[0]
    t = batch * seq
    if depth - n_conv != 1:
        raise NotImplementedError("exactly one MLA layer reads the shared K/V in this trunk")

    c_pad = jnp.pad(c, ((0, -batch % SUBLANES), (0, 0)))

    def split(m, n):
        return [m[:batch, None, i * d:(i + 1) * d] for i in range(n)]

    ada_mix = _ada(c_pad, w_ada_mix, b_ada_mix)
    ada_mlp = _ada(c_pad, w_ada_mlp, b_ada_mlp)
    kv_shift, kv_scale = split(_ada(c_pad, w_ada_kv[None], b_ada_kv[None])[0], 2)

    def row(v):
        return v.reshape(1, -1)

    inv = 1.0 / (ROPE_THETA ** (jnp.arange(0, QK_ROPE_DIM, 2, dtype=F32) / QK_ROPE_DIM))
    inv_col = inv.reshape(-1, 1)
    pos = positions.reshape(1, t)
    q_scale = QK_DIM ** -0.5 * math.log2(math.e)

    h = x.reshape(t, d)
    for l in range(depth):
        shift, scale, gate = split(ada_mix[l], 3)
        if l < n_conv:
            u = _conv_in(h, shift, scale, row(g_pre_mix[l]), conv_w_in, row(conv_b_in[l]), l, seq)
            dw8 = jnp.broadcast_to(conv_dw[l][:, None, :], (CONV_WIDTH, SUBLANES, d))
            h = _conv_out(u, dw8, row(conv_dw_b[l]), row(conv_ln_g[l]), row(conv_ln_b[l]),
                          conv_w_out[l].astype(BF16), row(conv_b_out[l]), h, gate, row(g_post_mix[l]), seq)
        else:
            j = l - n_conv
            w_dkvkr = jnp.concatenate(
                [w_dkv, w_kr, jnp.zeros((d, LANES - QK_ROPE_DIM), F32)], axis=1).astype(BF16)
            qt, k, vt = _proj(h, pos, inv_col, kv_shift, kv_scale, shift, scale, row(g_kv), row(g_pre_mix[l]),
                              w_dkvkr, row(g_ckv), w_uk.astype(BF16), w_uv.T.astype(BF16),
                              w_dq[j].astype(BF16), row(g_cq[j]), w_uq[j].T.astype(BF16), q_scale, batch, seq)
            a = _attention(qt, k, vt)
            h = _attn_out(a, w_o[j], h, gate, row(g_post_mix[l]), seq)
        shift, scale, gate = split(ada_mlp[l], 3)
        h = _mlp(h, shift, scale, gate, row(g_pre_mlp[l]), row(g_post_mlp[l]), mlp_w_up, mlp_w_down, l, seq)
    return h.reshape(batch, seq, d)
```

```python
import functools
import math

import jax
import jax.numpy as jnp
from jax import lax
from jax.experimental import pallas as pl
from jax.experimental.pallas import tpu as pltpu

F32 = jnp.float32
BF16 = jnp.bfloat16

EPS = 1e-6
NEG = -1e30
ROPE_THETA = 10000.0

N_HEADS = 16
QK_NOPE_DIM = 128
QK_ROPE_DIM = 64
QK_DIM = QK_NOPE_DIM + QK_ROPE_DIM
V_HEAD_DIM = 128
CONV_WIDTH = 31

LANES = 128
SUBLANES = 8
VMEM_BYTES_V7X = 64 * 1024 * 1024
VMEM_LIMIT = VMEM_BYTES_V7X - 8 * 1024 * 1024

ROW_CHUNK = 16
ROW_GROUP = 16
CONV_ROWS = 32
CONV_LANES = 256
HALO = 32
SHIFT_TILES = 5
MLP_TAIL_ROWS = 256
PROJ_ROWS = 512
VT_TILE = 256
ATTN_BLOCK = 512
ATTN_HEADS = 8


def _cparams(*sem):
    return pltpu.CompilerParams(dimension_semantics=sem, vmem_limit_bytes=VMEM_LIMIT)


def _dot(a, b):
    return jnp.dot(a, b, preferred_element_type=F32)


def _dot_nt(a, b):
    return lax.dot_general(a, b, (((1,), (1,)), ((), ())), preferred_element_type=F32)


def _sigmoid(x):
    return 1.0 / (1.0 + jnp.exp(-x))


def _row_loop(n_rows, body, row0=0, inline=False):
    span = ROW_CHUNK * ROW_GROUP
    if inline:
        for r0 in range(row0, row0 + n_rows, span):
            body([pl.ds(r0 + k * ROW_CHUNK, ROW_CHUNK) for k in range(ROW_GROUP)])
        return

    def step(i, carry):
        r0 = pl.multiple_of(row0 + i * span, span)
        body([pl.ds(r0 + k * ROW_CHUNK, ROW_CHUNK) for k in range(ROW_GROUP)])
        return carry
    lax.fori_loop(0, n_rows // span, step, 0)


def _inv_rms(x):
    return lax.rsqrt(jnp.mean(x * x, axis=-1, keepdims=True) + EPS)


def _ada_kernel(c_ref, w_ref, b_ref, o_ref):
    @pl.when(pl.program_id(1) == 0)
    def _():
        o_ref[0] = jnp.broadcast_to(b_ref[0], o_ref.shape[1:])

    c = c_ref[...]
    o_ref[0] += _dot((c * _sigmoid(c)).astype(BF16), w_ref[0].astype(BF16))


def _ada(c_pad, w, b, tk=256):
    nl, d, n = w.shape
    rows = c_pad.shape[0]
    return pl.pallas_call(
        _ada_kernel,
        grid=(nl, d // tk),
        in_specs=[
            pl.BlockSpec((rows, tk), lambda l, k: (0, k)),
            pl.BlockSpec((1, tk, n), lambda l, k: (l, k, 0)),
            pl.BlockSpec((1, 1, n), lambda l, k: (l, 0, 0)),
        ],
        out_specs=pl.BlockSpec((1, rows, n), lambda l, k: (l, 0, 0)),
        out_shape=jax.ShapeDtypeStruct((nl, rows, n), F32),
        compiler_params=_cparams("arbitrary", "arbitrary"),
        name="ada",
    )(c_pad, w, b.reshape(nl, 1, n))


def _norm_modulate_to(h_ref, g_ref, shift_ref, scale_ref, out_ref, n_rows, **loop_kw):
    mul = g_ref[...] * (1.0 + scale_ref[0])
    add = shift_ref[0]

    def body(chunks):
        for rows in chunks:
            x = h_ref[rows, :]
            out_ref[rows, :] = ((x * _inv_rms(x)) * mul + add).astype(BF16)
    _row_loop(n_rows, body, **loop_kw)


def _residual_gate_norm(h_ref, y_ref, gate_ref, g_ref, o_ref, n_rows, **loop_kw):
    mul = gate_ref[0] * g_ref[...]

    def body(chunks):
        ys = [y_ref[rows, :] for rows in chunks]
        scaled = [y * _inv_rms(y) for y in ys]
        for rows, s in zip(chunks, scaled):
            o_ref[rows, :] = h_ref[rows, :] + s * mul
    _row_loop(n_rows, body, **loop_kw)


def _vec_spec(d, seq_tiles):
    return pl.BlockSpec((1, 1, d), lambda m, *_: (m // seq_tiles, 0, 0))


def _row_spec(d):
    return pl.BlockSpec((1, d), lambda *_: (0, 0))


def _conv_in_kernel(h_ref, shift_ref, scale_ref, g_ref, wa_ref, wg_ref, ba_ref, bg_ref, u_ref, hn_ref):
    tm = h_ref.shape[0]

    def glu():
        hn = hn_ref[...]
        a = _dot(hn, wa_ref[...].astype(BF16)) + ba_ref[...]
        g = _dot(hn, wg_ref[...].astype(BF16)) + bg_ref[...]
        u_ref[...] = a * _sigmoid(g)

    @pl.when(pl.program_id(1) == 0)
    def _():
        _norm_modulate_to(h_ref, g_ref, shift_ref, scale_ref, hn_ref, tm, inline=True)
        glu()

    @pl.when(pl.program_id(1) != 0)
    def _():
        glu()


def _conv_in(h, shift, scale, g_pre, w_in, b_in, layer, seq, tm=1024, tn=512):
    t, d = h.shape
    nt = d // tn
    return pl.pallas_call(
        _conv_in_kernel,
        grid=(t // tm, nt),
        in_specs=[
            pl.BlockSpec((tm, d), lambda m, n: (m, 0)),
            _vec_spec(d, seq // tm), _vec_spec(d, seq // tm), _row_spec(d),
            pl.BlockSpec((None, d, tn), lambda m, n: (layer, 0, n)),
            pl.BlockSpec((None, d, tn), lambda m, n: (layer, 0, n + nt)),
            pl.BlockSpec((1, tn), lambda m, n: (0, n)),
            pl.BlockSpec((1, tn), lambda m, n: (0, n + nt)),
        ],
        out_specs=pl.BlockSpec((tm, tn), lambda m, n: (m, n)),
        out_shape=jax.ShapeDtypeStruct((t, d), F32),
        scratch_shapes=[pltpu.VMEM((tm, d), BF16)],
        compiler_params=_cparams("arbitrary", "arbitrary"),
        name="conv_in",
    )(h, shift, scale, g_pre, w_in, w_in, b_in, b_in)


def _conv_out_kernel(u_ref, halo_ref, dw_ref, dwb_ref, lng_ref, lnb_ref, wout_ref, bout_ref,
                     h_ref, gate_ref, gpost_ref, o_ref, sh_ref, cv_ref, a_ref, *, seq_tiles):
    tm, d = u_ref.shape
    first = (pl.program_id(0) % seq_tiles) == 0
    sh_ref[0, 0:HALO, :] = jnp.where(first, 0.0, halo_ref[...])
    sh_ref[0, HALO:, :] = u_ref[...]

    span = SHIFT_TILES * SUBLANES

    def shift_step(i, carry):
        r0 = pl.multiple_of(i * span, span)
        x = sh_ref[0, pl.ds(r0, span + SUBLANES), :]
        for b in range(1, SUBLANES):
            sh_ref[b, pl.ds(r0, span), :] = x[b:b + span]
        return carry
    lax.fori_loop(0, (tm + HALO - SUBLANES) // span, shift_step, 0)

    base = HALO - (CONV_WIDTH - 1)
    groups = CONV_ROWS // SUBLANES
    taps_by_shift = {}
    for j in range(CONV_WIDTH):
        a, b = divmod(base + j, SUBLANES)
        taps_by_shift.setdefault(b, []).append((a, j))

    def conv_step(i, carry):
        r0 = pl.multiple_of(i * CONV_ROWS, CONV_ROWS)
        for c in range(d // CONV_LANES):
            lanes = slice(c * CONV_LANES, (c + 1) * CONV_LANES)
            accs = [jnp.broadcast_to(dwb_ref[:, lanes], (SUBLANES, CONV_LANES))] * groups
            for b, taps in taps_by_shift.items():
                ws = {j: dw_ref[j, :, lanes] for _, j in taps}
                tiles = [a for a, _ in taps]
                for k in range(min(tiles), max(tiles) + groups):
                    x = sh_ref[b, pl.ds(r0 + k * SUBLANES, SUBLANES), lanes]
                    for a, j in taps:
                        if 0 <= k - a < groups:
                            accs[k - a] = accs[k - a] + x * ws[j]
            for g in range(groups):
                cv_ref[pl.ds(r0 + g * SUBLANES, SUBLANES), lanes] = accs[g]
        return carry
    lax.fori_loop(0, tm // CONV_ROWS, conv_step, 0)

    lng = lng_ref[...]
    lnb = lnb_ref[...]

    def ln_body(chunks):
        for rows in chunks:
            x = cv_ref[rows, :]
            mu = jnp.mean(x, axis=-1, keepdims=True)
            xc = x - mu
            var = jnp.mean(xc * xc, axis=-1, keepdims=True)
            y = (xc * lax.rsqrt(var + EPS)) * lng + lnb
            a_ref[rows, :] = (y * _sigmoid(y)).astype(BF16)
    _row_loop(tm, ln_body, inline=True)

    cv_ref[...] = _dot(a_ref[...], wout_ref[...]) + bout_ref[...]
    _residual_gate_norm(h_ref, cv_ref, gate_ref, gpost_ref, o_ref, tm, inline=True)


def _conv_out(u, dw8, dw_b, ln_g, ln_b, w_out, b_out, h, gate, g_post, seq, tm=256):
    t, d = u.shape
    seq_tiles = seq // tm
    halo_blocks = tm // HALO
    return pl.pallas_call(
        functools.partial(_conv_out_kernel, seq_tiles=seq_tiles),
        grid=(t // tm,),
        in_specs=[
            pl.BlockSpec((tm, d), lambda m: (m, 0)),
            pl.BlockSpec((HALO, d), lambda m: (jnp.maximum(m * halo_blocks - 1, 0), 0)),
            pl.BlockSpec(dw8.shape, lambda m: (0, 0, 0), pipeline_mode=pl.Buffered(1)),
            _row_spec(d), _row_spec(d), _row_spec(d),
            pl.BlockSpec((d, d), lambda m: (0, 0), pipeline_mode=pl.Buffered(1)),
            _row_spec(d),
            pl.BlockSpec((tm, d), lambda m: (m, 0)),
            _vec_spec(d, seq_tiles), _row_spec(d),
        ],
        out_specs=pl.BlockSpec((tm, d), lambda m: (m, 0)),
        out_shape=jax.ShapeDtypeStruct((t, d), F32),
        scratch_shapes=[pltpu.VMEM((SUBLANES, tm + HALO, d), F32), pltpu.VMEM((tm, d), F32),
                        pltpu.VMEM((tm, d), BF16)],
        compiler_params=_cparams("arbitrary"),
        name="conv_out",
    )(u, u, dw8, dw_b, ln_g, ln_b, w_out, b_out, h, gate, g_post)


def _mlp_kernel(h_ref, shift_ref, scale_ref, gate_ref, gpre_ref, gpost_ref, wup_ref, wdown_ref,
                o_ref, hn_ref):
    tm = h_ref.shape[0]
    f = pl.program_id(1)
    last = pl.num_programs(1) - 1

    def hidden():
        up = jnp.maximum(_dot(hn_ref[...], wup_ref[...].astype(BF16)), 0.0)
        return (up * up).astype(BF16)

    @pl.when(f == 0)
    def _():
        _norm_modulate_to(h_ref, gpre_ref, shift_ref, scale_ref, hn_ref, tm, inline=True)
        o_ref[...] = _dot(hidden(), wdown_ref[...].astype(BF16))

    @pl.when((f != 0) & (f != last))
    def _():
        o_ref[...] += _dot(hidden(), wdown_ref[...].astype(BF16))

    @pl.when(f == last)
    def _():
        hid = hidden()
        wdown = wdown_ref[...].astype(BF16)
        for r0 in range(0, tm, MLP_TAIL_ROWS):
            rows = slice(r0, r0 + MLP_TAIL_ROWS)
            o_ref[rows, :] += _dot(hid[rows], wdown)
            _residual_gate_norm(h_ref, o_ref, gate_ref, gpost_ref, o_ref, MLP_TAIL_ROWS, row0=r0, inline=True)


def _mlp(h, shift, scale, gate, g_pre, g_post, w_up, w_down, layer, seq, tm=1024, tf=512):
    t, d = h.shape
    ff = w_up.shape[2]
    seq_tiles = seq // tm
    assert ff // tf >= 2, "the kernel's first and last ff steps must be distinct"
    return pl.pallas_call(
        _mlp_kernel,
        grid=(t // tm, ff // tf),
        in_specs=[
            pl.BlockSpec((tm, d), lambda m, f: (m, 0), pipeline_mode=pl.Buffered(1)),
            _vec_spec(d, seq_tiles), _vec_spec(d, seq_tiles), _vec_spec(d, seq_tiles),
            _row_spec(d), _row_spec(d),
            pl.BlockSpec((None, d, tf), lambda m, f: (layer, 0, f)),
            pl.BlockSpec((None, tf, d), lambda m, f: (layer, f, 0)),
        ],
        out_specs=pl.BlockSpec((tm, d), lambda m, f: (m, 0)),
        out_shape=jax.ShapeDtypeStruct((t, d), F32),
        scratch_shapes=[pltpu.VMEM((tm, d), BF16)],
        compiler_params=_cparams("arbitrary", "arbitrary"),
        name="mlp",
    )(h, shift, scale, gate, g_pre, g_post, w_up, w_down)


def _rope_t(x, cos, sin):
    half = QK_ROPE_DIM // 2
    x1, x2 = x[:half], x[half:]
    return x1 * cos - x2 * sin, x2 * cos + x1 * sin


def _proj_kernel(h_ref, pos_ref, inv_ref, kvshift_ref, kvscale_ref, shift_ref, scale_ref,
                 gkv_ref, gpre_ref, wdkv_ref, gckv_ref, wuk_ref, wuvt_ref, wdq_ref, gcq_ref, wuqt_ref,
                 qt_ref, k_ref, vt_ref, kvn_ref, hn_ref, *, q_scale):
    tm = h_ref.shape[0]
    kv_mul = gkv_ref[...] * (1.0 + kvscale_ref[0])
    kv_add = kvshift_ref[0]
    q_mul = gpre_ref[...] * (1.0 + scale_ref[0])
    q_add = shift_ref[0]

    def norm_body(chunks):
        for rows in chunks:
            x = h_ref[rows, :]
            xn = x * _inv_rms(x)
            kvn_ref[rows, :] = (xn * kv_mul + kv_add).astype(BF16)
            hn_ref[rows, :] = (xn * q_mul + q_add).astype(BF16)
    _row_loop(tm, norm_body, inline=True)

    ang = inv_ref[...] * pos_ref[...].astype(F32)
    cos = jnp.cos(ang)
    sin = jnp.sin(ang)

    r_kv = gckv_ref.shape[1]
    t1 = _dot(kvn_ref[...], wdkv_ref[...])
    ckv = t1[:, :r_kv]
    ckv = ((ckv * _inv_rms(ckv)) * gckv_ref[...]).astype(BF16)
    kr1, kr2 = _rope_t(t1[:, r_kv:].T[:QK_ROPE_DIM], cos, sin)
    k_rope = jnp.concatenate([kr1, kr2, jnp.zeros((LANES - QK_ROPE_DIM, tm), F32)], axis=0).T
    k_rope = k_rope[:, :QK_ROPE_DIM].astype(BF16)
    k_nope = _dot(ckv, wuk_ref[...]).astype(BF16)
    vt = _dot_nt(wuvt_ref[...], ckv).astype(BF16)

    cq = _dot(hn_ref[...], wdq_ref[...])
    cq = ((cq * _inv_rms(cq)) * gcq_ref[...]).astype(BF16)
    qt = _dot_nt(wuqt_ref[...], cq) * q_scale
    for hd in range(N_HEADS):
        r0 = hd * QK_DIM
        q1, q2 = _rope_t(qt[r0 + QK_NOPE_DIM:r0 + QK_DIM], cos, sin)
        qt_ref[0, hd, :QK_NOPE_DIM, :] = qt[r0:r0 + QK_NOPE_DIM].astype(BF16)
        qt_ref[0, hd, QK_NOPE_DIM:QK_NOPE_DIM + QK_ROPE_DIM // 2, :] = q1.astype(BF16)
        qt_ref[0, hd, QK_NOPE_DIM + QK_ROPE_DIM // 2:, :] = q2.astype(BF16)
        k_ref[0, hd, :, :QK_NOPE_DIM] = k_nope[:, hd * QK_NOPE_DIM:(hd + 1) * QK_NOPE_DIM]
        k_ref[0, hd, :, QK_NOPE_DIM:] = k_rope
        for i in range(vt_ref.shape[2]):
            vt_ref[0, hd, i] = vt[hd * V_HEAD_DIM:(hd + 1) * V_HEAD_DIM, i * VT_TILE:(i + 1) * VT_TILE]


def _proj(h, pos, inv_col, kv_shift, kv_scale, shift, scale, g_kv, g_pre, w_dkvkr, g_ckv, w_uk, w_uvt,
          w_dq, g_cq, w_uqt, q_scale, batch, seq):
    t, d = h.shape
    tm = PROJ_ROWS
    seq_tiles = seq // tm
    r_kv = g_ckv.shape[1]
    r_q = g_cq.shape[1]

    def full(a):
        return pl.BlockSpec(a.shape, lambda m: (0,) * a.ndim, pipeline_mode=pl.Buffered(1))

    return pl.pallas_call(
        functools.partial(_proj_kernel, q_scale=q_scale),
        grid=(t // tm,),
        in_specs=[
            pl.BlockSpec((tm, d), lambda m: (m, 0)),
            pl.BlockSpec((1, tm), lambda m: (0, m)),
            full(inv_col),
            _vec_spec(d, seq_tiles), _vec_spec(d, seq_tiles), _vec_spec(d, seq_tiles), _vec_spec(d, seq_tiles),
            _row_spec(d), _row_spec(d),
            full(w_dkvkr), _row_spec(r_kv), full(w_uk), full(w_uvt), full(w_dq), _row_spec(r_q), full(w_uqt),
        ],
        out_specs=[
            pl.BlockSpec((1, N_HEADS, QK_DIM, tm), lambda m: (m // seq_tiles, 0, 0, m % seq_tiles)),
            pl.BlockSpec((1, N_HEADS, tm, QK_DIM), lambda m: (m // seq_tiles, 0, m % seq_tiles, 0)),
            pl.BlockSpec((1, N_HEADS, tm // VT_TILE, V_HEAD_DIM, VT_TILE),
                         lambda m: (m // seq_tiles, 0, m % seq_tiles, 0, 0)),
        ],
        out_shape=[
            jax.ShapeDtypeStruct((batch, N_HEADS, QK_DIM, seq), BF16),
            jax.ShapeDtypeStruct((batch, N_HEADS, seq, QK_DIM), BF16),
            jax.ShapeDtypeStruct((batch, N_HEADS, seq // VT_TILE, V_HEAD_DIM, VT_TILE), BF16),
        ],
        scratch_shapes=[pltpu.VMEM((tm, d), BF16), pltpu.VMEM((tm, d), BF16)],
        compiler_params=_cparams("arbitrary"),
        name="proj",
    )(h, pos, inv_col, kv_shift, kv_scale, shift, scale, g_kv, g_pre, w_dkvkr, g_ckv, w_uk, w_uvt, w_dq, g_cq, w_uqt)


def _attn_kernel(qt_ref, k_ref, vt_ref, o_ref, sa_ref, sb_ref, acc_ref, st_ref, *, blk):
    qi = pl.program_id(2)
    heads = range(qt_ref.shape[1])
    half = vt_ref.shape[-1]
    assert blk == 2 * half

    def scores_to(s_ref, hd, tile):
        k = k_ref[0, hd, pl.ds(pl.multiple_of(tile * half, half), half), :]
        s = _dot(k, qt_ref[0, hd])
        s_ref[hd] = s
        return jnp.max(s, axis=0, keepdims=True)

    def update(s, mx, hd, tile, m, l):
        m_new = jnp.maximum(m, mx)
        alpha = jnp.exp2(m - m_new)
        p = jnp.exp2(s - m_new)
        l = alpha * l + jnp.sum(p, axis=0, keepdims=True)
        acc_ref[hd] = alpha * acc_ref[hd] + _dot(vt_ref[0, hd, tile], p.astype(BF16))
        return m_new, l

    def body(j, carry):
        out = []
        for hd, (mxa, m, l) in zip(heads, carry):
            mxb = scores_to(sb_ref, hd, 2 * j + 1)
            m, l = update(sa_ref[hd], mxa, hd, 2 * j, m, l)
            mxa = scores_to(sa_ref, hd, 2 * j + 2)
            m, l = update(sb_ref[hd], mxb, hd, 2 * j + 1, m, l)
            out.append((mxa, m, l))
        return tuple(out)

    acc_ref[...] = jnp.zeros_like(acc_ref)
    init = tuple((scores_to(sa_ref, hd, 0), jnp.full((1, blk), NEG, F32), jnp.zeros((1, blk), F32))
                 for hd in heads)
    carry = lax.fori_loop(0, qi, body, init)

    key = lax.broadcasted_iota(jnp.int32, (half, blk), 0)
    qry = lax.broadcasted_iota(jnp.int32, (half, blk), 1)
    key_sq = lax.broadcasted_iota(jnp.int32, (half, half), 0)
    qry_sq = lax.broadcasted_iota(jnp.int32, (half, half), 1)
    for hd, (_, m, l) in zip(heads, carry):
        kb = k_ref[0, hd, pl.ds(pl.multiple_of((2 * qi + 1) * half, half), half), :]
        sb = jnp.where(key_sq <= qry_sq, _dot(kb, qt_ref[0, hd, :, half:]), NEG)
        sa = jnp.where(key <= qry, sa_ref[hd], NEG)
        m, l = update(sa, jnp.max(sa, axis=0, keepdims=True), hd, 2 * qi, m, l)
        st_ref[0:1, :] = m
        st_ref[1:2, :] = l
        m_new = jnp.maximum(st_ref[0:1, half:], jnp.max(sb, axis=0, keepdims=True))
        alpha = jnp.exp2(st_ref[0:1, half:] - m_new)
        p = jnp.exp2(sb - m_new)
        st_ref[1:2, half:] = alpha * st_ref[1:2, half:] + jnp.sum(p, axis=0, keepdims=True)
        acc_ref[hd, :, half:] = alpha * acc_ref[hd, :, half:] + _dot(vt_ref[0, hd, 2 * qi + 1], p.astype(BF16))
        o_ref[0, hd] = (acc_ref[hd] / st_ref[1:2, :]).T.astype(BF16)


def _attention(qt, k, vt, blk=ATTN_BLOCK, nh=ATTN_HEADS):
    b, n_heads, s, _ = k.shape
    half = vt.shape[-1]
    return pl.pallas_call(
        functools.partial(_attn_kernel, blk=blk),
        grid=(b, n_heads // nh, s // blk),
        in_specs=[
            pl.BlockSpec((1, nh, QK_DIM, blk), lambda bi, hi, qi: (bi, hi, 0, qi)),
            pl.BlockSpec((1, nh, s, QK_DIM), lambda bi, hi, qi: (bi, hi, 0, 0)),
            pl.BlockSpec((1, nh) + vt.shape[2:], lambda bi, hi, qi: (bi, hi, 0, 0, 0)),
        ],
        out_specs=pl.BlockSpec((1, nh, blk, V_HEAD_DIM), lambda bi, hi, qi: (bi, hi, qi, 0)),
        out_shape=jax.ShapeDtypeStruct((b, n_heads, s, V_HEAD_DIM), BF16),
        scratch_shapes=[pltpu.VMEM((nh, half, blk), F32), pltpu.VMEM((nh, half, blk), F32),
                        pltpu.VMEM((nh, V_HEAD_DIM, blk), F32), pltpu.VMEM((SUBLANES, blk), F32)],
        compiler_params=_cparams("arbitrary", "arbitrary", "arbitrary"),
        name="attn",
    )(qt, k, vt)


def _attn_out_kernel(a_ref, wo_ref, h_ref, gate_ref, gpost_ref, o_ref, cat_ref, wbf_ref):
    tm = h_ref.shape[0]

    @pl.when(pl.program_id(0) == 0)
    def _():
        wbf_ref[...] = wo_ref[...].astype(BF16)

    for hd in range(N_HEADS):
        cat_ref[:, hd * V_HEAD_DIM:(hd + 1) * V_HEAD_DIM] = a_ref[0, hd]
    o_ref[...] = _dot(cat_ref[...], wbf_ref[...])
    _residual_gate_norm(h_ref, o_ref, gate_ref, gpost_ref, o_ref, tm, inline=True)


def _attn_out(a, w_o, h, gate, g_post, seq, tm=256):
    t, d = h.shape
    seq_tiles = seq // tm
    return pl.pallas_call(
        _attn_out_kernel,
        grid=(t // tm,),
        in_specs=[
            pl.BlockSpec((1, N_HEADS, tm, V_HEAD_DIM), lambda m: (m // seq_tiles, 0, m % seq_tiles, 0)),
            pl.BlockSpec(w_o.shape, lambda m: (0, 0), pipeline_mode=pl.Buffered(1)),
            pl.BlockSpec((tm, d), lambda m: (m, 0)),
            _vec_spec(d, seq_tiles), _row_spec(d),
        ],
        out_specs=pl.BlockSpec((tm, d), lambda m: (m, 0)),
        out_shape=jax.ShapeDtypeStruct((t, d), F32),
        scratch_shapes=[pltpu.VMEM((tm, N_HEADS * V_HEAD_DIM), BF16), pltpu.VMEM(w_o.shape, BF16)],
        compiler_params=_cparams("arbitrary"),
        name="attn_out",
    )(a, w_o, h, gate, g_post)


def kernel(x, c, positions, w_ada_mix, b_ada_mix, w_ada_mlp, b_ada_mlp, g_pre_mix, g_post_mix, g_pre_mlp,
           g_post_mlp, conv_w_in, conv_b_in, conv_dw, conv_dw_b, conv_ln_g, conv_ln_b, conv_w_out, conv_b_out,
           w_ada_kv, b_ada_kv, g_kv, w_dkv, g_ckv, w_kr, w_uk, w_uv, w_dq, g_cq, w_uq, w_o, mlp_w_up,
           mlp_w_down):
    batch, seq, d = x.shape
    depth = w_ada_mix.shape[0]
    n_conv = conv_w_in.shape[0]
    t = batch * seq
    if depth - n_conv != 1:
        raise NotImplementedError("exactly one MLA layer reads the shared K/V in this trunk")

    c_pad = jnp.pad(c, ((0, -batch % SUBLANES), (0, 0)))

    def split(m, n):
        return [m[:batch, None, i * d:(i + 1) * d] for i in range(n)]

    ada_mix = _ada(c_pad, w_ada_mix, b_ada_mix)
    ada_mlp = _ada(c_pad, w_ada_mlp, b_ada_mlp)
    kv_shift, kv_scale = split(_ada(c_pad, w_ada_kv[None], b_ada_kv[None])[0], 2)

    def row(v):
        return v.reshape(1, -1)

    inv = 1.0 / (ROPE_THETA ** (jnp.arange(0, QK_ROPE_DIM, 2, dtype=F32) / QK_ROPE_DIM))
    inv_col = inv.reshape(-1, 1)
    pos = positions.reshape(1, t)
    q_scale = QK_DIM ** -0.5 * math.log2(math.e)

    h = x.reshape(t, d)
    for l in range(depth):
        shift, scale, gate = split(ada_mix[l], 3)
        if l < n_conv:
            u = _conv_in(h, shift, scale, row(g_pre_mix[l]), conv_w_in, row(conv_b_in[l]), l, seq)
            dw8 = jnp.broadcast_to(conv_dw[l][:, None, :], (CONV_WIDTH, SUBLANES, d))
            h = _conv_out(u, dw8, row(conv_dw_b[l]), row(conv_ln_g[l]), row(conv_ln_b[l]),
                          conv_w_out[l].astype(BF16), row(conv_b_out[l]), h, gate, row(g_post_mix[l]), seq)
        else:
            j = l - n_conv
            w_dkvkr = jnp.concatenate(
                [w_dkv, w_kr, jnp.zeros((d, LANES - QK_ROPE_DIM), F32)], axis=1).astype(BF16)
            qt, k, vt = _proj(h, pos, inv_col, kv_shift, kv_scale, shift, scale, row(g_kv), row(g_pre_mix[l]),
                              w_dkvkr, row(g_ckv), w_uk.astype(BF16), w_uv.T.astype(BF16),
                              w_dq[j].astype(BF16), row(g_cq[j]), w_uq[j].T.astype(BF16), q_scale, batch, seq)
            a = _attention(qt, k, vt)
            h = _attn_out(a, w_o[j], h, gate, row(g_post_mix[l]), seq)
        shift, scale, gate = split(ada_mlp[l], 3)
        h = _mlp(h, shift, scale, gate, row(g_pre_mlp[l]), row(g_post_mlp[l]), mlp_w_up, mlp_w_down, l, seq)
    return h.reshape(batch, seq, d)
```

```python
import functools
import math

import jax
import jax.numpy as jnp
from jax import lax
from jax.experimental import pallas as pl
from jax.experimental.pallas import tpu as pltpu

F32 = jnp.float32
BF16 = jnp.bfloat16

EPS = 1e-6
NEG = -1e30
ROPE_THETA = 10000.0

N_HEADS = 16
QK_NOPE_DIM = 128
QK_ROPE_DIM = 64
QK_DIM = QK_NOPE_DIM + QK_ROPE_DIM
V_HEAD_DIM = 128
CONV_WIDTH = 31

LANES = 128
SUBLANES = 8
VMEM_BYTES_V7X = 64 * 1024 * 1024
VMEM_LIMIT = VMEM_BYTES_V7X - 8 * 1024 * 1024

ROW_CHUNK = 16
ROW_GROUP = 16
GLU_COLS = 256
CONV_ROWS = 32
CONV_LANES = 256
HALO = 32
SHIFT_TILES = 5
MLP_TAIL_ROWS = 256
PROJ_ROWS = 512
VT_TILE = 256
ATTN_BLOCK = 512
ATTN_HEADS = 8


def _cparams(*sem):
    return pltpu.CompilerParams(dimension_semantics=sem, vmem_limit_bytes=VMEM_LIMIT)


def _dot(a, b):
    return jnp.dot(a, b, preferred_element_type=F32)


def _dot_nt(a, b):
    return lax.dot_general(a, b, (((1,), (1,)), ((), ())), preferred_element_type=F32)


def _sigmoid(x):
    return 1.0 / (1.0 + jnp.exp(-x))


def _row_loop(n_rows, body, row0=0, inline=False):
    span = ROW_CHUNK * ROW_GROUP
    if inline:
        for r0 in range(row0, row0 + n_rows, span):
            body([pl.ds(r0 + k * ROW_CHUNK, ROW_CHUNK) for k in range(ROW_GROUP)])
        return

    def step(i, carry):
        r0 = pl.multiple_of(row0 + i * span, span)
        body([pl.ds(r0 + k * ROW_CHUNK, ROW_CHUNK) for k in range(ROW_GROUP)])
        return carry
    lax.fori_loop(0, n_rows // span, step, 0)


def _inv_rms(x):
    return lax.rsqrt(jnp.mean(x * x, axis=-1, keepdims=True) + EPS)


def _ada_kernel(c_ref, w_ref, b_ref, o_ref):
    @pl.when(pl.program_id(1) == 0)
    def _():
        o_ref[0] = jnp.broadcast_to(b_ref[0], o_ref.shape[1:])

    c = c_ref[...]
    o_ref[0] += _dot((c * _sigmoid(c)).astype(BF16), w_ref[0].astype(BF16))


def _ada(c_pad, w, b, tk=256):
    nl, d, n = w.shape
    rows = c_pad.shape[0]
    return pl.pallas_call(
        _ada_kernel,
        grid=(nl, d // tk),
        in_specs=[
            pl.BlockSpec((rows, tk), lambda l, k: (0, k)),
            pl.BlockSpec((1, tk, n), lambda l, k: (l, k, 0)),
            pl.BlockSpec((1, 1, n), lambda l, k: (l, 0, 0)),
        ],
        out_specs=pl.BlockSpec((1, rows, n), lambda l, k: (l, 0, 0)),
        out_shape=jax.ShapeDtypeStruct((nl, rows, n), F32),
        compiler_params=_cparams("arbitrary", "arbitrary"),
        name="ada",
    )(c_pad, w, b.reshape(nl, 1, n))


def _norm_modulate_to(h_ref, g_ref, shift_ref, scale_ref, out_ref, n_rows, **loop_kw):
    mul = g_ref[...] * (1.0 + scale_ref[0])
    add = shift_ref[0]

    def body(chunks):
        for rows in chunks:
            x = h_ref[rows, :]
            out_ref[rows, :] = ((x * _inv_rms(x)) * mul + add).astype(BF16)
    _row_loop(n_rows, body, **loop_kw)


def _residual_gate_norm(h_ref, y_ref, gate_ref, g_ref, o_ref, n_rows, **loop_kw):
    mul = gate_ref[0] * g_ref[...]

    def body(chunks):
        ys = [y_ref[rows, :] for rows in chunks]
        scaled = [y * _inv_rms(y) for y in ys]
        for rows, s in zip(chunks, scaled):
            o_ref[rows, :] = h_ref[rows, :] + s * mul
    _row_loop(n_rows, body, **loop_kw)


def _vec_spec(d, seq_tiles):
    return pl.BlockSpec((1, 1, d), lambda m, *_: (m // seq_tiles, 0, 0))


def _row_spec(d):
    return pl.BlockSpec((1, d), lambda *_: (0, 0))


def _conv_in_kernel(h_ref, shift_ref, scale_ref, g_ref, wa_ref, wg_ref, ba_ref, bg_ref, u_ref, hn_ref):
    tm = h_ref.shape[0]

    def glu():
        hn = hn_ref[...]
        for c0 in range(0, wa_ref.shape[1], GLU_COLS):
            cols = slice(c0, c0 + GLU_COLS)
            a = _dot(hn, wa_ref[:, cols].astype(BF16)) + ba_ref[:, cols]
            g = _dot(hn, wg_ref[:, cols].astype(BF16)) + bg_ref[:, cols]
            u_ref[:, cols] = a * _sigmoid(g)

    @pl.when(pl.program_id(1) == 0)
    def _():
        _norm_modulate_to(h_ref, g_ref, shift_ref, scale_ref, hn_ref, tm, inline=True)
        glu()

    @pl.when(pl.program_id(1) != 0)
    def _():
        glu()


def _conv_in(h, shift, scale, g_pre, w_in, b_in, layer, seq, tm=1024, tn=512):
    t, d = h.shape
    nt = d // tn
    return pl.pallas_call(
        _conv_in_kernel,
        grid=(t // tm, nt),
        in_specs=[
            pl.BlockSpec((tm, d), lambda m, n: (m, 0)),
            _vec_spec(d, seq // tm), _vec_spec(d, seq // tm), _row_spec(d),
            pl.BlockSpec((None, d, tn), lambda m, n: (layer, 0, n)),
            pl.BlockSpec((None, d, tn), lambda m, n: (layer, 0, n + nt)),
            pl.BlockSpec((1, tn), lambda m, n: (0, n)),
            pl.BlockSpec((1, tn), lambda m, n: (0, n + nt)),
        ],
        out_specs=pl.BlockSpec((tm, tn), lambda m, n: (m, n)),
        out_shape=jax.ShapeDtypeStruct((t, d), F32),
        scratch_shapes=[pltpu.VMEM((tm, d), BF16)],
        compiler_params=_cparams("arbitrary", "arbitrary"),
        name="conv_in",
    )(h, shift, scale, g_pre, w_in, w_in, b_in, b_in)


def _conv_out_kernel(u_ref, halo_ref, dw_ref, dwb_ref, lng_ref, lnb_ref, wout_ref, bout_ref,
                     h_ref, gate_ref, gpost_ref, o_ref, sh_ref, cv_ref, a_ref, *, seq_tiles):
    tm, d = u_ref.shape
    first = (pl.program_id(0) % seq_tiles) == 0
    sh_ref[0, 0:HALO, :] = jnp.where(first, 0.0, halo_ref[...])
    sh_ref[0, HALO:, :] = u_ref[...]

    span = SHIFT_TILES * SUBLANES

    def shift_step(i, carry):
        r0 = pl.multiple_of(i * span, span)
        x = sh_ref[0, pl.ds(r0, span + SUBLANES), :]
        for b in range(1, SUBLANES):
            sh_ref[b, pl.ds(r0, span), :] = x[b:b + span]
        return carry
    lax.fori_loop(0, (tm + HALO - SUBLANES) // span, shift_step, 0)

    base = HALO - (CONV_WIDTH - 1)
    groups = CONV_ROWS // SUBLANES
    taps_by_shift = {}
    for j in range(CONV_WIDTH):
        a, b = divmod(base + j, SUBLANES)
        taps_by_shift.setdefault(b, []).append((a, j))

    def conv_step(i, carry):
        r0 = pl.multiple_of(i * CONV_ROWS, CONV_ROWS)
        for c in range(d // CONV_LANES):
            lanes = slice(c * CONV_LANES, (c + 1) * CONV_LANES)
            accs = [jnp.broadcast_to(dwb_ref[:, lanes], (SUBLANES, CONV_LANES))] * groups
            for b, taps in taps_by_shift.items():
                ws = {j: dw_ref[j, :, lanes] for _, j in taps}
                tiles = [a for a, _ in taps]
                for k in range(min(tiles), max(tiles) + groups):
                    x = sh_ref[b, pl.ds(r0 + k * SUBLANES, SUBLANES), lanes]
                    for a, j in taps:
                        if 0 <= k - a < groups:
                            accs[k - a] = accs[k - a] + x * ws[j]
            for g in range(groups):
                cv_ref[pl.ds(r0 + g * SUBLANES, SUBLANES), lanes] = accs[g]
        return carry
    lax.fori_loop(0, tm // CONV_ROWS, conv_step, 0)

    lng = lng_ref[...]
    lnb = lnb_ref[...]

    def ln_body(chunks):
        for rows in chunks:
            x = cv_ref[rows, :]
            mu = jnp.mean(x, axis=-1, keepdims=True)
            xc = x - mu
            var = jnp.mean(xc * xc, axis=-1, keepdims=True)
            y = (xc * lax.rsqrt(var + EPS)) * lng + lnb
            a_ref[rows, :] = (y * _sigmoid(y)).astype(BF16)
    _row_loop(tm, ln_body, inline=True)

    cv_ref[...] = _dot(a_ref[...], wout_ref[...]) + bout_ref[...]
    _residual_gate_norm(h_ref, cv_ref, gate_ref, gpost_ref, o_ref, tm, inline=True)


def _conv_out(u, dw8, dw_b, ln_g, ln_b, w_out, b_out, h, gate, g_post, seq, tm=256):
    t, d = u.shape
    seq_tiles = seq // tm
    halo_blocks = tm // HALO
    return pl.pallas_call(
        functools.partial(_conv_out_kernel, seq_tiles=seq_tiles),
        grid=(t // tm,),
        in_specs=[
            pl.BlockSpec((tm, d), lambda m: (m, 0)),
            pl.BlockSpec((HALO, d), lambda m: (jnp.maximum(m * halo_blocks - 1, 0), 0)),
            pl.BlockSpec(dw8.shape, lambda m: (0, 0, 0), pipeline_mode=pl.Buffered(1)),
            _row_spec(d), _row_spec(d), _row_spec(d),
            pl.BlockSpec((d, d), lambda m: (0, 0), pipeline_mode=pl.Buffered(1)),
            _row_spec(d),
            pl.BlockSpec((tm, d), lambda m: (m, 0)),
            _vec_spec(d, seq_tiles), _row_spec(d),
        ],
        out_specs=pl.BlockSpec((tm, d), lambda m: (m, 0)),
        out_shape=jax.ShapeDtypeStruct((t, d), F32),
        scratch_shapes=[pltpu.VMEM((SUBLANES, tm + HALO, d), F32), pltpu.VMEM((tm, d), F32),
                        pltpu.VMEM((tm, d), BF16)],
        compiler_params=_cparams("arbitrary"),
        name="conv_out",
    )(u, u, dw8, dw_b, ln_g, ln_b, w_out, b_out, h, gate, g_post)


def _mlp_kernel(h_ref, shift_ref, scale_ref, gate_ref, gpre_ref, gpost_ref, wup_ref, wdown_ref,
                o_ref, hn_ref):
    tm = h_ref.shape[0]
    f = pl.program_id(1)
    last = pl.num_programs(1) - 1

    def hidden():
        up = jnp.maximum(_dot(hn_ref[...], wup_ref[...].astype(BF16)), 0.0)
        return (up * up).astype(BF16)

    @pl.when(f == 0)
    def _():
        _norm_modulate_to(h_ref, gpre_ref, shift_ref, scale_ref, hn_ref, tm, inline=True)
        o_ref[...] = _dot(hidden(), wdown_ref[...].astype(BF16))

    @pl.when((f != 0) & (f != last))
    def _():
        o_ref[...] += _dot(hidden(), wdown_ref[...].astype(BF16))

    @pl.when(f == last)
    def _():
        hid = hidden()
        wdown = wdown_ref[...].astype(BF16)
        for r0 in range(0, tm, MLP_TAIL_ROWS):
            rows = slice(r0, r0 + MLP_TAIL_ROWS)
            o_ref[rows, :] += _dot(hid[rows], wdown)
            _residual_gate_norm(h_ref, o_ref, gate_ref, gpost_ref, o_ref, MLP_TAIL_ROWS, row0=r0, inline=True)


def _mlp(h, shift, scale, gate, g_pre, g_post, w_up, w_down, layer, seq, tm=1024, tf=512):
    t, d = h.shape
    ff = w_up.shape[2]
    seq_tiles = seq // tm
    assert ff // tf >= 2, "the kernel's first and last ff steps must be distinct"
    return pl.pallas_call(
        _mlp_kernel,
        grid=(t // tm, ff // tf),
        in_specs=[
            pl.BlockSpec((tm, d), lambda m, f: (m, 0), pipeline_mode=pl.Buffered(1)),
            _vec_spec(d, seq_tiles), _vec_spec(d, seq_tiles), _vec_spec(d, seq_tiles),
            _row_spec(d), _row_spec(d),
            pl.BlockSpec((None, d, tf), lambda m, f: (layer, 0, f)),
            pl.BlockSpec((None, tf, d), lambda m, f: (layer, f, 0)),
        ],
        out_specs=pl.BlockSpec((tm, d), lambda m, f: (m, 0)),
        out_shape=jax.ShapeDtypeStruct((t, d), F32),
        scratch_shapes=[pltpu.VMEM((tm, d), BF16)],
        compiler_params=_cparams("arbitrary", "arbitrary"),
        name="mlp",
    )(h, shift, scale, gate, g_pre, g_post, w_up, w_down)


def _rope_t(x, cos, sin):
    half = QK_ROPE_DIM // 2
    x1, x2 = x[:half], x[half:]
    return x1 * cos - x2 * sin, x2 * cos + x1 * sin


def _proj_kernel(h_ref, pos_ref, inv_ref, kvshift_ref, kvscale_ref, shift_ref, scale_ref,
                 gkv_ref, gpre_ref, wdkv_ref, gckv_ref, wuk_ref, wuvt_ref, wdq_ref, gcq_ref, wuqt_ref,
                 qt_ref, k_ref, vt_ref, kvn_ref, hn_ref, *, q_scale):
    tm = h_ref.shape[0]
    kv_mul = gkv_ref[...] * (1.0 + kvscale_ref[0])
    kv_add = kvshift_ref[0]
    q_mul = gpre_ref[...] * (1.0 + scale_ref[0])
    q_add = shift_ref[0]

    def norm_body(chunks):
        for rows in chunks:
            x = h_ref[rows, :]
            xn = x * _inv_rms(x)
            kvn_ref[rows, :] = (xn * kv_mul + kv_add).astype(BF16)
            hn_ref[rows, :] = (xn * q_mul + q_add).astype(BF16)
    _row_loop(tm, norm_body, inline=True)

    ang = inv_ref[...] * pos_ref[...].astype(F32)
    cos = jnp.cos(ang)
    sin = jnp.sin(ang)

    r_kv = gckv_ref.shape[1]
    t1 = _dot(kvn_ref[...], wdkv_ref[...])
    ckv = t1[:, :r_kv]
    ckv = ((ckv * _inv_rms(ckv)) * gckv_ref[...]).astype(BF16)
    kr1, kr2 = _rope_t(t1[:, r_kv:].T[:QK_ROPE_DIM], cos, sin)
    k_rope = jnp.concatenate([kr1, kr2, jnp.zeros((LANES - QK_ROPE_DIM, tm), F32)], axis=0).T
    k_rope = k_rope[:, :QK_ROPE_DIM].astype(BF16)
    k_nope = _dot(ckv, wuk_ref[...]).astype(BF16)
    vt = _dot_nt(wuvt_ref[...], ckv).astype(BF16)

    cq = _dot(hn_ref[...], wdq_ref[...])
    cq = ((cq * _inv_rms(cq)) * gcq_ref[...]).astype(BF16)
    qt = _dot_nt(wuqt_ref[...], cq) * q_scale
    for hd in range(N_HEADS):
        r0 = hd * QK_DIM
        q1, q2 = _rope_t(qt[r0 + QK_NOPE_DIM:r0 + QK_DIM], cos, sin)
        qt_ref[0, hd, :QK_NOPE_DIM, :] = qt[r0:r0 + QK_NOPE_DIM].astype(BF16)
        qt_ref[0, hd, QK_NOPE_DIM:QK_NOPE_DIM + QK_ROPE_DIM // 2, :] = q1.astype(BF16)
        qt_ref[0, hd, QK_NOPE_DIM + QK_ROPE_DIM // 2:, :] = q2.astype(BF16)
        k_ref[0, hd, :, :QK_NOPE_DIM] = k_nope[:, hd * QK_NOPE_DIM:(hd + 1) * QK_NOPE_DIM]
        k_ref[0, hd, :, QK_NOPE_DIM:] = k_rope
        for i in range(vt_ref.shape[2]):
            vt_ref[0, hd, i] = vt[hd * V_HEAD_DIM:(hd + 1) * V_HEAD_DIM, i * VT_TILE:(i + 1) * VT_TILE]


def _proj(h, pos, inv_col, kv_shift, kv_scale, shift, scale, g_kv, g_pre, w_dkvkr, g_ckv, w_uk, w_uvt,
          w_dq, g_cq, w_uqt, q_scale, batch, seq):
    t, d = h.shape
    tm = PROJ_ROWS
    seq_tiles = seq // tm
    r_kv = g_ckv.shape[1]
    r_q = g_cq.shape[1]

    def full(a):
        return pl.BlockSpec(a.shape, lambda m: (0,) * a.ndim, pipeline_mode=pl.Buffered(1))

    return pl.pallas_call(
        functools.partial(_proj_kernel, q_scale=q_scale),
        grid=(t // tm,),
        in_specs=[
            pl.BlockSpec((tm, d), lambda m: (m, 0)),
            pl.BlockSpec((1, tm), lambda m: (0, m)),
            full(inv_col),
            _vec_spec(d, seq_tiles), _vec_spec(d, seq_tiles), _vec_spec(d, seq_tiles), _vec_spec(d, seq_tiles),
            _row_spec(d), _row_spec(d),
            full(w_dkvkr), _row_spec(r_kv), full(w_uk), full(w_uvt), full(w_dq), _row_spec(r_q), full(w_uqt),
        ],
        out_specs=[
            pl.BlockSpec((1, N_HEADS, QK_DIM, tm), lambda m: (m // seq_tiles, 0, 0, m % seq_tiles)),
            pl.BlockSpec((1, N_HEADS, tm, QK_DIM), lambda m: (m // seq_tiles, 0, m % seq_tiles, 0)),
            pl.BlockSpec((1, N_HEADS, tm // VT_TILE, V_HEAD_DIM, VT_TILE),
                         lambda m: (m // seq_tiles, 0, m % seq_tiles, 0, 0)),
        ],
        out_shape=[
            jax.ShapeDtypeStruct((batch, N_HEADS, QK_DIM, seq), BF16),
            jax.ShapeDtypeStruct((batch, N_HEADS, seq, QK_DIM), BF16),
            jax.ShapeDtypeStruct((batch, N_HEADS, seq // VT_TILE, V_HEAD_DIM, VT_TILE), BF16),
        ],
        scratch_shapes=[pltpu.VMEM((tm, d), BF16), pltpu.VMEM((tm, d), BF16)],
        compiler_params=_cparams("arbitrary"),
        name="proj",
    )(h, pos, inv_col, kv_shift, kv_scale, shift, scale, g_kv, g_pre, w_dkvkr, g_ckv, w_uk, w_uvt, w_dq, g_cq, w_uqt)


def _attn_kernel(qt_ref, k_ref, vt_ref, o_ref, sa_ref, sb_ref, acc_ref, st_ref, *, blk):
    qi = pl.program_id(2)
    heads = range(qt_ref.shape[1])
    half = vt_ref.shape[-1]
    assert blk == 2 * half

    def scores_to(s_ref, hd, tile):
        k = k_ref[0, hd, pl.ds(pl.multiple_of(tile * half, half), half), :]
        s = _dot(k, qt_ref[0, hd])
        s_ref[hd] = s
        return jnp.max(s, axis=0, keepdims=True)

    def update(s, mx, hd, tile, m, l):
        m_new = jnp.maximum(m, mx)
        alpha = jnp.exp2(m - m_new)
        p = jnp.exp2(s - m_new)
        l = alpha * l + jnp.sum(p, axis=0, keepdims=True)
        acc_ref[hd] = alpha * acc_ref[hd] + _dot(vt_ref[0, hd, tile], p.astype(BF16))
        return m_new, l

    def body(j, carry):
        out = []
        for hd, (mxa, m, l) in zip(heads, carry):
            mxb = scores_to(sb_ref, hd, 2 * j + 1)
            m, l = update(sa_ref[hd], mxa, hd, 2 * j, m, l)
            mxa = scores_to(sa_ref, hd, 2 * j + 2)
            m, l = update(sb_ref[hd], mxb, hd, 2 * j + 1, m, l)
            out.append((mxa, m, l))
        return tuple(out)

    acc_ref[...] = jnp.zeros_like(acc_ref)
    init = tuple((scores_to(sa_ref, hd, 0), jnp.full((1, blk), NEG, F32), jnp.zeros((1, blk), F32))
                 for hd in heads)
    carry = lax.fori_loop(0, qi, body, init)

    key = lax.broadcasted_iota(jnp.int32, (half, blk), 0)
    qry = lax.broadcasted_iota(jnp.int32, (half, blk), 1)
    key_sq = lax.broadcasted_iota(jnp.int32, (half, half), 0)
    qry_sq = lax.broadcasted_iota(jnp.int32, (half, half), 1)
    for hd, (_, m, l) in zip(heads, carry):
        kb = k_ref[0, hd, pl.ds(pl.multiple_of((2 * qi + 1) * half, half), half), :]
        sb = jnp.where(key_sq <= qry_sq, _dot(kb, qt_ref[0, hd, :, half:]), NEG)
        sa = jnp.where(key <= qry, sa_ref[hd], NEG)
        m, l = update(sa, jnp.max(sa, axis=0, keepdims=True), hd, 2 * qi, m, l)
        st_ref[0:1, :] = m
        st_ref[1:2, :] = l
        m_new = jnp.maximum(st_ref[0:1, half:], jnp.max(sb, axis=0, keepdims=True))
        alpha = jnp.exp2(st_ref[0:1, half:] - m_new)
        p = jnp.exp2(sb - m_new)
        st_ref[1:2, half:] = alpha * st_ref[1:2, half:] + jnp.sum(p, axis=0, keepdims=True)
        acc_ref[hd, :, half:] = alpha * acc_ref[hd, :, half:] + _dot(vt_ref[0, hd, 2 * qi + 1], p.astype(BF16))
        o_ref[0, hd] = (acc_ref[hd] / st_ref[1:2, :]).T.astype(BF16)


def _attention(qt, k, vt, blk=ATTN_BLOCK, nh=ATTN_HEADS):
    b, n_heads, s, _ = k.shape
    half = vt.shape[-1]
    return pl.pallas_call(
        functools.partial(_attn_kernel, blk=blk),
        grid=(b, n_heads // nh, s // blk),
        in_specs=[
            pl.BlockSpec((1, nh, QK_DIM, blk), lambda bi, hi, qi: (bi, hi, 0, qi)),
            pl.BlockSpec((1, nh, s, QK_DIM), lambda bi, hi, qi: (bi, hi, 0, 0)),
            pl.BlockSpec((1, nh) + vt.shape[2:], lambda bi, hi, qi: (bi, hi, 0, 0, 0)),
        ],
        out_specs=pl.BlockSpec((1, nh, blk, V_HEAD_DIM), lambda bi, hi, qi: (bi, hi, qi, 0)),
        out_shape=jax.ShapeDtypeStruct((b, n_heads, s, V_HEAD_DIM), BF16),
        scratch_shapes=[pltpu.VMEM((nh, half, blk), F32), pltpu.VMEM((nh, half, blk), F32),
                        pltpu.VMEM((nh, V_HEAD_DIM, blk), F32), pltpu.VMEM((SUBLANES, blk), F32)],
        compiler_params=_cparams("arbitrary", "arbitrary", "arbitrary"),
        name="attn",
    )(qt, k, vt)


def _attn_out_kernel(a_ref, wo_ref, h_ref, gate_ref, gpost_ref, o_ref, cat_ref, wbf_ref):
    tm = h_ref.shape[0]

    @pl.when(pl.program_id(0) == 0)
    def _():
        wbf_ref[...] = wo_ref[...].astype(BF16)

    for hd in range(N_HEADS):
        cat_ref[:, hd * V_HEAD_DIM:(hd + 1) * V_HEAD_DIM] = a_ref[0, hd]
    o_ref[...] = _dot(cat_ref[...], wbf_ref[...])
    _residual_gate_norm(h_ref, o_ref, gate_ref, gpost_ref, o_ref, tm, inline=True)


def _attn_out(a, w_o, h, gate, g_post, seq, tm=256):
    t, d = h.shape
    seq_tiles = seq // tm
    return pl.pallas_call(
        _attn_out_kernel,
        grid=(t // tm,),
        in_specs=[
            pl.BlockSpec((1, N_HEADS, tm, V_HEAD_DIM), lambda m: (m // seq_tiles, 0, m % seq_tiles, 0)),
            pl.BlockSpec(w_o.shape, lambda m: (0, 0), pipeline_mode=pl.Buffered(1)),
            pl.BlockSpec((tm, d), lambda m: (m, 0)),
            _vec_spec(d, seq_tiles), _row_spec(d),
        ],
        out_specs=pl.BlockSpec((tm, d), lambda m: (m, 0)),
        out_shape=jax.ShapeDtypeStruct((t, d), F32),
        scratch_shapes=[pltpu.VMEM((tm, N_HEADS * V_HEAD_DIM), BF16), pltpu.VMEM(w_o.shape, BF16)],
        compiler_params=_cparams("arbitrary"),
        name="attn_out",
    )(a, w_o, h, gate, g_post)


def kernel(x, c, positions, w_ada_mix, b_ada_mix, w_ada_mlp, b_ada_mlp, g_pre_mix, g_post_mix, g_pre_mlp,
           g_post_mlp, conv_w_in, conv_b_in, conv_dw, conv_dw_b, conv_ln_g, conv_ln_b, conv_w_out, conv_b_out,
           w_ada_kv, b_ada_kv, g_kv, w_dkv, g_ckv, w_kr, w_uk, w_uv, w_dq, g_cq, w_uq, w_o, mlp_w_up,
           mlp_w_down):
    batch, seq, d = x.shape
    depth = w_ada_mix.shape[0]
    n_conv = conv_w_in.shape[0]
    t = batch * seq
    if depth - n_conv != 1:
        raise NotImplementedError("exactly one MLA layer reads the shared K/V in this trunk")

    c_pad = jnp.pad(c, ((0, -batch % SUBLANES), (0, 0)))

    def split(m, n):
        return [m[:batch, None, i * d:(i + 1) * d] for i in range(n)]

    ada_mix = _ada(c_pad, w_ada_mix, b_ada_mix)
    ada_mlp = _ada(c_pad, w_ada_mlp, b_ada_mlp)
    kv_shift, kv_scale = split(_ada(c_pad, w_ada_kv[None], b_ada_kv[None])[0], 2)

    def row(v):
        return v.reshape(1, -1)

    inv = 1.0 / (ROPE_THETA ** (jnp.arange(0, QK_ROPE_DIM, 2, dtype=F32) / QK_ROPE_DIM))
    inv_col = inv.reshape(-1, 1)
    pos = positions.reshape(1, t)
    q_scale = QK_DIM ** -0.5 * math.log2(math.e)

    h = x.reshape(t, d)
    for l in range(depth):
        shift, scale, gate = split(ada_mix[l], 3)
        if l < n_conv:
            u = _conv_in(h, shift, scale, row(g_pre_mix[l]), conv_w_in, row(conv_b_in[l]), l, seq)
            dw8 = jnp.broadcast_to(conv_dw[l][:, None, :], (CONV_WIDTH, SUBLANES, d))
            h = _conv_out(u, dw8, row(conv_dw_b[l]), row(conv_ln_g[l]), row(conv_ln_b[l]),
                          conv_w_out[l].astype(BF16), row(conv_b_out[l]), h, gate, row(g_post_mix[l]), seq)
        else:
            j = l - n_conv
            w_dkvkr = jnp.concatenate(
                [w_dkv, w_kr, jnp.zeros((d, LANES - QK_ROPE_DIM), F32)], axis=1).astype(BF16)
            qt, k, vt = _proj(h, pos, inv_col, kv_shift, kv_scale, shift, scale, row(g_kv), row(g_pre_mix[l]),
                              w_dkvkr, row(g_ckv), w_uk.astype(BF16), w_uv.T.astype(BF16),
                              w_dq[j].astype(BF16), row(g_cq[j]), w_uq[j].T.astype(BF16), q_scale, batch, seq)
            a = _attention(qt, k, vt)
            h = _attn_out(a, w_o[j], h, gate, row(g_post_mix[l]), seq)
        shift, scale, gate = split(ada_mlp[l], 3)
        h = _mlp(h, shift, scale, gate, row(g_pre_mlp[l]), row(g_post_mlp[l]), mlp_w_up, mlp_w_down, l, seq)
    return h.reshape(batch, seq, d)
```

```python
import functools
import math

import jax
import jax.numpy as jnp
from jax import lax
from jax.experimental import pallas as pl
from jax.experimental.pallas import tpu as pltpu

F32 = jnp.float32
BF16 = jnp.bfloat16

EPS = 1e-6
NEG = -1e30
ROPE_THETA = 10000.0

N_HEADS = 16
QK_NOPE_DIM = 128
QK_ROPE_DIM = 64
QK_DIM = QK_NOPE_DIM + QK_ROPE_DIM
V_HEAD_DIM = 128
CONV_WIDTH = 31

LANES = 128
SUBLANES = 8
VMEM_BYTES_V7X = 64 * 1024 * 1024
VMEM_LIMIT = VMEM_BYTES_V7X - 8 * 1024 * 1024

ROW_CHUNK = 16
ROW_GROUP = 16
GLU_COLS = 256
CONV_ROWS = 128
CONV_LANES = 128
HALO = 32
SHIFT_TILES = 5
MLP_TAIL_ROWS = 256
PROJ_ROWS = 512
VT_TILE = 256
ATTN_BLOCK = 512
ATTN_HEADS = 8


def _cparams(*sem):
    return pltpu.CompilerParams(dimension_semantics=sem, vmem_limit_bytes=VMEM_LIMIT)


def _dot(a, b):
    return jnp.dot(a, b, preferred_element_type=F32)


def _dot_nt(a, b):
    return lax.dot_general(a, b, (((1,), (1,)), ((), ())), preferred_element_type=F32)


def _sigmoid(x):
    return 1.0 / (1.0 + jnp.exp(-x))


def _row_loop(n_rows, body, row0=0, inline=False):
    span = ROW_CHUNK * ROW_GROUP
    if inline:
        for r0 in range(row0, row0 + n_rows, span):
            body([pl.ds(r0 + k * ROW_CHUNK, ROW_CHUNK) for k in range(ROW_GROUP)])
        return

    def step(i, carry):
        r0 = pl.multiple_of(row0 + i * span, span)
        body([pl.ds(r0 + k * ROW_CHUNK, ROW_CHUNK) for k in range(ROW_GROUP)])
        return carry
    lax.fori_loop(0, n_rows // span, step, 0)


def _inv_rms(x):
    return lax.rsqrt(jnp.mean(x * x, axis=-1, keepdims=True) + EPS)


def _ada_kernel(c_ref, w_ref, b_ref, o_ref):
    @pl.when(pl.program_id(1) == 0)
    def _():
        o_ref[0] = jnp.broadcast_to(b_ref[0], o_ref.shape[1:])

    c = c_ref[...]
    o_ref[0] += _dot((c * _sigmoid(c)).astype(BF16), w_ref[0].astype(BF16))


def _ada(c_pad, w, b, tk=256):
    nl, d, n = w.shape
    rows = c_pad.shape[0]
    return pl.pallas_call(
        _ada_kernel,
        grid=(nl, d // tk),
        in_specs=[
            pl.BlockSpec((rows, tk), lambda l, k: (0, k)),
            pl.BlockSpec((1, tk, n), lambda l, k: (l, k, 0)),
            pl.BlockSpec((1, 1, n), lambda l, k: (l, 0, 0)),
        ],
        out_specs=pl.BlockSpec((1, rows, n), lambda l, k: (l, 0, 0)),
        out_shape=jax.ShapeDtypeStruct((nl, rows, n), F32),
        compiler_params=_cparams("arbitrary", "arbitrary"),
        name="ada",
    )(c_pad, w, b.reshape(nl, 1, n))


def _norm_modulate_to(h_ref, g_ref, shift_ref, scale_ref, out_ref, n_rows, **loop_kw):
    mul = g_ref[...] * (1.0 + scale_ref[0])
    add = shift_ref[0]

    def body(chunks):
        for rows in chunks:
            x = h_ref[rows, :]
            out_ref[rows, :] = ((x * _inv_rms(x)) * mul + add).astype(BF16)
    _row_loop(n_rows, body, **loop_kw)


def _residual_gate_norm(h_ref, y_ref, gate_ref, g_ref, o_ref, n_rows, **loop_kw):
    mul = gate_ref[0] * g_ref[...]

    def body(chunks):
        ys = [y_ref[rows, :] for rows in chunks]
        scaled = [y * _inv_rms(y) for y in ys]
        for rows, s in zip(chunks, scaled):
            o_ref[rows, :] = h_ref[rows, :] + s * mul
    _row_loop(n_rows, body, **loop_kw)


def _vec_spec(d, seq_tiles):
    return pl.BlockSpec((1, 1, d), lambda m, *_: (m // seq_tiles, 0, 0))


def _row_spec(d):
    return pl.BlockSpec((1, d), lambda *_: (0, 0))


def _conv_in_kernel(h_ref, shift_ref, scale_ref, g_ref, wa_ref, wg_ref, ba_ref, bg_ref, u_ref, hn_ref):
    tm = h_ref.shape[0]

    def glu():
        hn = hn_ref[...]
        for c0 in range(0, wa_ref.shape[1], GLU_COLS):
            cols = slice(c0, c0 + GLU_COLS)
            a = _dot(hn, wa_ref[:, cols].astype(BF16)) + ba_ref[:, cols]
            g = _dot(hn, wg_ref[:, cols].astype(BF16)) + bg_ref[:, cols]
            u_ref[:, cols] = a * _sigmoid(g)

    @pl.when(pl.program_id(1) == 0)
    def _():
        _norm_modulate_to(h_ref, g_ref, shift_ref, scale_ref, hn_ref, tm, inline=True)
        glu()

    @pl.when(pl.program_id(1) != 0)
    def _():
        glu()


def _conv_in(h, shift, scale, g_pre, w_in, b_in, layer, seq, tm=1024, tn=512):
    t, d = h.shape
    nt = d // tn
    return pl.pallas_call(
        _conv_in_kernel,
        grid=(t // tm, nt),
        in_specs=[
            pl.BlockSpec((tm, d), lambda m, n: (m, 0)),
            _vec_spec(d, seq // tm), _vec_spec(d, seq // tm), _row_spec(d),
            pl.BlockSpec((None, d, tn), lambda m, n: (layer, 0, n)),
            pl.BlockSpec((None, d, tn), lambda m, n: (layer, 0, n + nt)),
            pl.BlockSpec((1, tn), lambda m, n: (0, n)),
            pl.BlockSpec((1, tn), lambda m, n: (0, n + nt)),
        ],
        out_specs=pl.BlockSpec((tm, tn), lambda m, n: (m, n)),
        out_shape=jax.ShapeDtypeStruct((t, d), F32),
        scratch_shapes=[pltpu.VMEM((tm, d), BF16)],
        compiler_params=_cparams("arbitrary", "arbitrary"),
        name="conv_in",
    )(h, shift, scale, g_pre, w_in, w_in, b_in, b_in)


def _conv_out_kernel(u_ref, halo_ref, dw_ref, dwb_ref, lng_ref, lnb_ref, wout_ref, bout_ref,
                     h_ref, gate_ref, gpost_ref, o_ref, sh_ref, cv_ref, a_ref, *, seq_tiles):
    tm, d = u_ref.shape
    first = (pl.program_id(0) % seq_tiles) == 0
    sh_ref[0, 0:HALO, :] = jnp.where(first, 0.0, halo_ref[...])
    sh_ref[0, HALO:, :] = u_ref[...]

    span = SHIFT_TILES * SUBLANES

    def shift_step(i, carry):
        r0 = pl.multiple_of(i * span, span)
        x = sh_ref[0, pl.ds(r0, span + SUBLANES), :]
        for b in range(1, SUBLANES):
            sh_ref[b, pl.ds(r0, span), :] = x[b:b + span]
        return carry
    lax.fori_loop(0, (tm + HALO - SUBLANES) // span, shift_step, 0)

    base = HALO - (CONV_WIDTH - 1)
    groups = CONV_ROWS // SUBLANES
    taps_by_shift = {}
    for j in range(CONV_WIDTH):
        a, b = divmod(base + j, SUBLANES)
        taps_by_shift.setdefault(b, []).append((a, j))

    def conv_step(i, carry):
        r0 = pl.multiple_of(i * CONV_ROWS, CONV_ROWS)
        for c in range(d // CONV_LANES):
            lanes = slice(c * CONV_LANES, (c + 1) * CONV_LANES)
            accs = [jnp.broadcast_to(dwb_ref[:, lanes], (SUBLANES, CONV_LANES))] * groups
            for b, taps in taps_by_shift.items():
                ws = {j: dw_ref[j, :, lanes] for _, j in taps}
                tiles = [a for a, _ in taps]
                for k in range(min(tiles), max(tiles) + groups):
                    x = sh_ref[b, pl.ds(r0 + k * SUBLANES, SUBLANES), lanes]
                    for a, j in taps:
                        if 0 <= k - a < groups:
                            accs[k - a] = accs[k - a] + x * ws[j]
            for g in range(groups):
                cv_ref[pl.ds(r0 + g * SUBLANES, SUBLANES), lanes] = accs[g]
        return carry
    lax.fori_loop(0, tm // CONV_ROWS, conv_step, 0)

    lng = lng_ref[...]
    lnb = lnb_ref[...]

    def ln_body(chunks):
        for rows in chunks:
            x = cv_ref[rows, :]
            mu = jnp.mean(x, axis=-1, keepdims=True)
            xc = x - mu
            var = jnp.mean(xc * xc, axis=-1, keepdims=True)
            y = (xc * lax.rsqrt(var + EPS)) * lng + lnb
            a_ref[rows, :] = (y * _sigmoid(y)).astype(BF16)
    _row_loop(tm, ln_body, inline=True)

    cv_ref[...] = _dot(a_ref[...], wout_ref[...]) + bout_ref[...]
    _residual_gate_norm(h_ref, cv_ref, gate_ref, gpost_ref, o_ref, tm, inline=True)


def _conv_out(u, dw8, dw_b, ln_g, ln_b, w_out, b_out, h, gate, g_post, seq, tm=256):
    t, d = u.shape
    seq_tiles = seq // tm
    halo_blocks = tm // HALO
    return pl.pallas_call(
        functools.partial(_conv_out_kernel, seq_tiles=seq_tiles),
        grid=(t // tm,),
        in_specs=[
            pl.BlockSpec((tm, d), lambda m: (m, 0)),
            pl.BlockSpec((HALO, d), lambda m: (jnp.maximum(m * halo_blocks - 1, 0), 0)),
            pl.BlockSpec(dw8.shape, lambda m: (0, 0, 0), pipeline_mode=pl.Buffered(1)),
            _row_spec(d), _row_spec(d), _row_spec(d),
            pl.BlockSpec((d, d), lambda m: (0, 0), pipeline_mode=pl.Buffered(1)),
            _row_spec(d),
            pl.BlockSpec((tm, d), lambda m: (m, 0)),
            _vec_spec(d, seq_tiles), _row_spec(d),
        ],
        out_specs=pl.BlockSpec((tm, d), lambda m: (m, 0)),
        out_shape=jax.ShapeDtypeStruct((t, d), F32),
        scratch_shapes=[pltpu.VMEM((SUBLANES, tm + HALO, d), F32), pltpu.VMEM((tm, d), F32),
                        pltpu.VMEM((tm, d), BF16)],
        compiler_params=_cparams("arbitrary"),
        name="conv_out",
    )(u, u, dw8, dw_b, ln_g, ln_b, w_out, b_out, h, gate, g_post)


def _mlp_kernel(h_ref, shift_ref, scale_ref, gate_ref, gpre_ref, gpost_ref, wup_ref, wdown_ref,
                o_ref, hn_ref):
    tm = h_ref.shape[0]
    f = pl.program_id(1)
    last = pl.num_programs(1) - 1

    def hidden():
        up = jnp.maximum(_dot(hn_ref[...], wup_ref[...].astype(BF16)), 0.0)
        return (up * up).astype(BF16)

    @pl.when(f == 0)
    def _():
        _norm_modulate_to(h_ref, gpre_ref, shift_ref, scale_ref, hn_ref, tm, inline=True)
        o_ref[...] = _dot(hidden(), wdown_ref[...].astype(BF16))

    @pl.when((f != 0) & (f != last))
    def _():
        o_ref[...] += _dot(hidden(), wdown_ref[...].astype(BF16))

    @pl.when(f == last)
    def _():
        hid = hidden()
        wdown = wdown_ref[...].astype(BF16)
        for r0 in range(0, tm, MLP_TAIL_ROWS):
            rows = slice(r0, r0 + MLP_TAIL_ROWS)
            o_ref[rows, :] += _dot(hid[rows], wdown)
            _residual_gate_norm(h_ref, o_ref, gate_ref, gpost_ref, o_ref, MLP_TAIL_ROWS, row0=r0, inline=True)


def _mlp(h, shift, scale, gate, g_pre, g_post, w_up, w_down, layer, seq, tm=1024, tf=512):
    t, d = h.shape
    ff = w_up.shape[2]
    seq_tiles = seq // tm
    assert ff // tf >= 2, "the kernel's first and last ff steps must be distinct"
    return pl.pallas_call(
        _mlp_kernel,
        grid=(t // tm, ff // tf),
        in_specs=[
            pl.BlockSpec((tm, d), lambda m, f: (m, 0), pipeline_mode=pl.Buffered(1)),
            _vec_spec(d, seq_tiles), _vec_spec(d, seq_tiles), _vec_spec(d, seq_tiles),
            _row_spec(d), _row_spec(d),
            pl.BlockSpec((None, d, tf), lambda m, f: (layer, 0, f)),
            pl.BlockSpec((None, tf, d), lambda m, f: (layer, f, 0)),
        ],
        out_specs=pl.BlockSpec((tm, d), lambda m, f: (m, 0)),
        out_shape=jax.ShapeDtypeStruct((t, d), F32),
        scratch_shapes=[pltpu.VMEM((tm, d), BF16)],
        compiler_params=_cparams("arbitrary", "arbitrary"),
        name="mlp",
    )(h, shift, scale, gate, g_pre, g_post, w_up, w_down)


def _rope_t(x, cos, sin):
    half = QK_ROPE_DIM // 2
    x1, x2 = x[:half], x[half:]
    return x1 * cos - x2 * sin, x2 * cos + x1 * sin


def _proj_kernel(h_ref, pos_ref, inv_ref, kvshift_ref, kvscale_ref, shift_ref, scale_ref,
                 gkv_ref, gpre_ref, wdkv_ref, gckv_ref, wuk_ref, wuvt_ref, wdq_ref, gcq_ref, wuqt_ref,
                 qt_ref, k_ref, vt_ref, kvn_ref, hn_ref, *, q_scale):
    tm = h_ref.shape[0]
    kv_mul = gkv_ref[...] * (1.0 + kvscale_ref[0])
    kv_add = kvshift_ref[0]
    q_mul = gpre_ref[...] * (1.0 + scale_ref[0])
    q_add = shift_ref[0]

    def norm_body(chunks):
        for rows in chunks:
            x = h_ref[rows, :]
            xn = x * _inv_rms(x)
            kvn_ref[rows, :] = (xn * kv_mul + kv_add).astype(BF16)
            hn_ref[rows, :] = (xn * q_mul + q_add).astype(BF16)
    _row_loop(tm, norm_body, inline=True)

    ang = inv_ref[...] * pos_ref[...].astype(F32)
    cos = jnp.cos(ang)
    sin = jnp.sin(ang)

    r_kv = gckv_ref.shape[1]
    t1 = _dot(kvn_ref[...], wdkv_ref[...])
    ckv = t1[:, :r_kv]
    ckv = ((ckv * _inv_rms(ckv)) * gckv_ref[...]).astype(BF16)
    kr1, kr2 = _rope_t(t1[:, r_kv:].T[:QK_ROPE_DIM], cos, sin)
    k_rope = jnp.concatenate([kr1, kr2, jnp.zeros((LANES - QK_ROPE_DIM, tm), F32)], axis=0).T
    k_rope = k_rope[:, :QK_ROPE_DIM].astype(BF16)
    k_nope = _dot(ckv, wuk_ref[...]).astype(BF16)
    vt = _dot_nt(wuvt_ref[...], ckv).astype(BF16)

    cq = _dot(hn_ref[...], wdq_ref[...])
    cq = ((cq * _inv_rms(cq)) * gcq_ref[...]).astype(BF16)
    qt = _dot_nt(wuqt_ref[...], cq) * q_scale
    for hd in range(N_HEADS):
        r0 = hd * QK_DIM
        q1, q2 = _rope_t(qt[r0 + QK_NOPE_DIM:r0 + QK_DIM], cos, sin)
        qt_ref[0, hd, :QK_NOPE_DIM, :] = qt[r0:r0 + QK_NOPE_DIM].astype(BF16)
        qt_ref[0, hd, QK_NOPE_DIM:QK_NOPE_DIM + QK_ROPE_DIM // 2, :] = q1.astype(BF16)
        qt_ref[0, hd, QK_NOPE_DIM + QK_ROPE_DIM // 2:, :] = q2.astype(BF16)
        k_ref[0, hd, :, :QK_NOPE_DIM] = k_nope[:, hd * QK_NOPE_DIM:(hd + 1) * QK_NOPE_DIM]
        k_ref[0, hd, :, QK_NOPE_DIM:] = k_rope
        for i in range(vt_ref.shape[2]):
            vt_ref[0, hd, i] = vt[hd * V_HEAD_DIM:(hd + 1) * V_HEAD_DIM, i * VT_TILE:(i + 1) * VT_TILE]


def _proj(h, pos, inv_col, kv_shift, kv_scale, shift, scale, g_kv, g_pre, w_dkvkr, g_ckv, w_uk, w_uvt,
          w_dq, g_cq, w_uqt, q_scale, batch, seq):
    t, d = h.shape
    tm = PROJ_ROWS
    seq_tiles = seq // tm
    r_kv = g_ckv.shape[1]
    r_q = g_cq.shape[1]

    def full(a):
        return pl.BlockSpec(a.shape, lambda m: (0,) * a.ndim, pipeline_mode=pl.Buffered(1))

    return pl.pallas_call(
        functools.partial(_proj_kernel, q_scale=q_scale),
        grid=(t // tm,),
        in_specs=[
            pl.BlockSpec((tm, d), lambda m: (m, 0)),
            pl.BlockSpec((1, tm), lambda m: (0, m)),
            full(inv_col),
            _vec_spec(d, seq_tiles), _vec_spec(d, seq_tiles), _vec_spec(d, seq_tiles), _vec_spec(d, seq_tiles),
            _row_spec(d), _row_spec(d),
            full(w_dkvkr), _row_spec(r_kv), full(w_uk), full(w_uvt), full(w_dq), _row_spec(r_q), full(w_uqt),
        ],
        out_specs=[
            pl.BlockSpec((1, N_HEADS, QK_DIM, tm), lambda m: (m // seq_tiles, 0, 0, m % seq_tiles)),
            pl.BlockSpec((1, N_HEADS, tm, QK_DIM), lambda m: (m // seq_tiles, 0, m % seq_tiles, 0)),
            pl.BlockSpec((1, N_HEADS, tm // VT_TILE, V_HEAD_DIM, VT_TILE),
                         lambda m: (m // seq_tiles, 0, m % seq_tiles, 0, 0)),
        ],
        out_shape=[
            jax.ShapeDtypeStruct((batch, N_HEADS, QK_DIM, seq), BF16),
            jax.ShapeDtypeStruct((batch, N_HEADS, seq, QK_DIM), BF16),
            jax.ShapeDtypeStruct((batch, N_HEADS, seq // VT_TILE, V_HEAD_DIM, VT_TILE), BF16),
        ],
        scratch_shapes=[pltpu.VMEM((tm, d), BF16), pltpu.VMEM((tm, d), BF16)],
        compiler_params=_cparams("arbitrary"),
        name="proj",
    )(h, pos, inv_col, kv_shift, kv_scale, shift, scale, g_kv, g_pre, w_dkvkr, g_ckv, w_uk, w_uvt, w_dq, g_cq, w_uqt)


def _attn_kernel(qt_ref, k_ref, vt_ref, o_ref, sa_ref, sb_ref, acc_ref, st_ref, *, blk):
    qi = pl.program_id(2)
    heads = range(qt_ref.shape[1])
    half = vt_ref.shape[-1]
    assert blk == 2 * half

    def scores_to(s_ref, hd, tile):
        k = k_ref[0, hd, pl.ds(pl.multiple_of(tile * half, half), half), :]
        s = _dot(k, qt_ref[0, hd])
        s_ref[hd] = s
        return jnp.max(s, axis=0, keepdims=True)

    def update(s, mx, hd, tile, m, l):
        m_new = jnp.maximum(m, mx)
        alpha = jnp.exp2(m - m_new)
        p = jnp.exp2(s - m_new)
        l = alpha * l + jnp.sum(p, axis=0, keepdims=True)
        acc_ref[hd] = alpha * acc_ref[hd] + _dot(vt_ref[0, hd, tile], p.astype(BF16))
        return m_new, l

    def body(j, carry):
        out = []
        for hd, (mxa, m, l) in zip(heads, carry):
            mxb = scores_to(sb_ref, hd, 2 * j + 1)
            m, l = update(sa_ref[hd], mxa, hd, 2 * j, m, l)
            mxa = scores_to(sa_ref, hd, 2 * j + 2)
            m, l = update(sb_ref[hd], mxb, hd, 2 * j + 1, m, l)
            out.append((mxa, m, l))
        return tuple(out)

    acc_ref[...] = jnp.zeros_like(acc_ref)
    init = tuple((scores_to(sa_ref, hd, 0), jnp.full((1, blk), NEG, F32), jnp.zeros((1, blk), F32))
                 for hd in heads)
    carry = lax.fori_loop(0, qi, body, init)

    key = lax.broadcasted_iota(jnp.int32, (half, blk), 0)
    qry = lax.broadcasted_iota(jnp.int32, (half, blk), 1)
    key_sq = lax.broadcasted_iota(jnp.int32, (half, half), 0)
    qry_sq = lax.broadcasted_iota(jnp.int32, (half, half), 1)
    for hd, (_, m, l) in zip(heads, carry):
        kb = k_ref[0, hd, pl.ds(pl.multiple_of((2 * qi + 1) * half, half), half), :]
        sb = jnp.where(key_sq <= qry_sq, _dot(kb, qt_ref[0, hd, :, half:]), NEG)
        sa = jnp.where(key <= qry, sa_ref[hd], NEG)
        m, l = update(sa, jnp.max(sa, axis=0, keepdims=True), hd, 2 * qi, m, l)
        st_ref[0:1, :] = m
        st_ref[1:2, :] = l
        m_new = jnp.maximum(st_ref[0:1, half:], jnp.max(sb, axis=0, keepdims=True))
        alpha = jnp.exp2(st_ref[0:1, half:] - m_new)
        p = jnp.exp2(sb - m_new)
        st_ref[1:2, half:] = alpha * st_ref[1:2, half:] + jnp.sum(p, axis=0, keepdims=True)
        acc_ref[hd, :, half:] = alpha * acc_ref[hd, :, half:] + _dot(vt_ref[0, hd, 2 * qi + 1], p.astype(BF16))
        o_ref[0, :, hd * V_HEAD_DIM:(hd + 1) * V_HEAD_DIM] = (acc_ref[hd] / st_ref[1:2, :]).T.astype(BF16)


def _attention(qt, k, vt, blk=ATTN_BLOCK, nh=ATTN_HEADS):
    b, n_heads, s, _ = k.shape
    half = vt.shape[-1]
    return pl.pallas_call(
        functools.partial(_attn_kernel, blk=blk),
        grid=(b, n_heads // nh, s // blk),
        in_specs=[
            pl.BlockSpec((1, nh, QK_DIM, blk), lambda bi, hi, qi: (bi, hi, 0, qi)),
            pl.BlockSpec((1, nh, s, QK_DIM), lambda bi, hi, qi: (bi, hi, 0, 0)),
            pl.BlockSpec((1, nh) + vt.shape[2:], lambda bi, hi, qi: (bi, hi, 0, 0, 0)),
        ],
        out_specs=pl.BlockSpec((1, blk, nh * V_HEAD_DIM), lambda bi, hi, qi: (bi, qi, hi)),
        out_shape=jax.ShapeDtypeStruct((b, s, n_heads * V_HEAD_DIM), BF16),
        scratch_shapes=[pltpu.VMEM((nh, half, blk), F32), pltpu.VMEM((nh, half, blk), F32),
                        pltpu.VMEM((nh, V_HEAD_DIM, blk), F32), pltpu.VMEM((SUBLANES, blk), F32)],
        compiler_params=_cparams("arbitrary", "arbitrary", "arbitrary"),
        name="attn",
    )(qt, k, vt)


def _attn_out_kernel(a_ref, wo_ref, h_ref, gate_ref, gpost_ref, o_ref, wbf_ref):
    tm = h_ref.shape[0]

    @pl.when(pl.program_id(0) == 0)
    def _():
        wbf_ref[...] = wo_ref[...].astype(BF16)

    o_ref[...] = _dot(a_ref[...], wbf_ref[...])
    _residual_gate_norm(h_ref, o_ref, gate_ref, gpost_ref, o_ref, tm, inline=True)


def _attn_out(a, w_o, h, gate, g_post, seq, tm=256):
    t, d = h.shape
    seq_tiles = seq // tm
    return pl.pallas_call(
        _attn_out_kernel,
        grid=(t // tm,),
        in_specs=[
            pl.BlockSpec((tm, a.shape[1]), lambda m: (m, 0)),
            pl.BlockSpec(w_o.shape, lambda m: (0, 0), pipeline_mode=pl.Buffered(1)),
            pl.BlockSpec((tm, d), lambda m: (m, 0)),
            _vec_spec(d, seq_tiles), _row_spec(d),
        ],
        out_specs=pl.BlockSpec((tm, d), lambda m: (m, 0)),
        out_shape=jax.ShapeDtypeStruct((t, d), F32),
        scratch_shapes=[pltpu.VMEM(w_o.shape, BF16)],
        compiler_params=_cparams("arbitrary"),
        name="attn_out",
    )(a, w_o, h, gate, g_post)


def kernel(x, c, positions, w_ada_mix, b_ada_mix, w_ada_mlp, b_ada_mlp, g_pre_mix, g_post_mix, g_pre_mlp,
           g_post_mlp, conv_w_in, conv_b_in, conv_dw, conv_dw_b, conv_ln_g, conv_ln_b, conv_w_out, conv_b_out,
           w_ada_kv, b_ada_kv, g_kv, w_dkv, g_ckv, w_kr, w_uk, w_uv, w_dq, g_cq, w_uq, w_o, mlp_w_up,
           mlp_w_down):
    batch, seq, d = x.shape
    depth = w_ada_mix.shape[0]
    n_conv = conv_w_in.shape[0]
    t = batch * seq
    if depth - n_conv != 1:
        raise NotImplementedError("exactly one MLA layer reads the shared K/V in this trunk")

    c_pad = jnp.pad(c, ((0, -batch % SUBLANES), (0, 0)))

    def split(m, n):
        return [m[:batch, None, i * d:(i + 1) * d] for i in range(n)]

    ada_mix = _ada(c_pad, w_ada_mix, b_ada_mix)
    ada_mlp = _ada(c_pad, w_ada_mlp, b_ada_mlp)
    kv_shift, kv_scale = split(_ada(c_pad, w_ada_kv[None], b_ada_kv[None])[0], 2)

    def row(v):
        return v.reshape(1, -1)

    inv = 1.0 / (ROPE_THETA ** (jnp.arange(0, QK_ROPE_DIM, 2, dtype=F32) / QK_ROPE_DIM))
    inv_col = inv.reshape(-1, 1)
    pos = positions.reshape(1, t)
    q_scale = QK_DIM ** -0.5 * math.log2(math.e)

    h = x.reshape(t, d)
    for l in range(depth):
        shift, scale, gate = split(ada_mix[l], 3)
        if l < n_conv:
            u = _conv_in(h, shift, scale, row(g_pre_mix[l]), conv_w_in, row(conv_b_in[l]), l, seq)
            dw8 = jnp.broadcast_to(conv_dw[l][:, None, :], (CONV_WIDTH, SUBLANES, d))
            h = _conv_out(u, dw8, row(conv_dw_b[l]), row(conv_ln_g[l]), row(conv_ln_b[l]),
                          conv_w_out[l].astype(BF16), row(conv_b_out[l]), h, gate, row(g_post_mix[l]), seq)
        else:
            j = l - n_conv
            w_dkvkr = jnp.concatenate(
                [w_dkv, w_kr, jnp.zeros((d, LANES - QK_ROPE_DIM), F32)], axis=1).astype(BF16)
            qt, k, vt = _proj(h, pos, inv_col, kv_shift, kv_scale, shift, scale, row(g_kv), row(g_pre_mix[l]),
                              w_dkvkr, row(g_ckv), w_uk.astype(BF16), w_uv.T.astype(BF16),
                              w_dq[j].astype(BF16), row(g_cq[j]), w_uq[j].T.astype(BF16), q_scale, batch, seq)
            a = _attention(qt, k, vt)
            h = _attn_out(a.reshape(t, -1), w_o[j], h, gate, row(g_post_mix[l]), seq)
        shift, scale, gate = split(ada_mlp[l], 3)
        h = _mlp(h, shift, scale, gate, row(g_pre_mlp[l]), row(g_post_mlp[l]), mlp_w_up, mlp_w_down, l, seq)
    return h.reshape(batch, seq, d)
```

```python
import functools
import math

import jax
import jax.numpy as jnp
from jax import lax
from jax.experimental import pallas as pl
from jax.experimental.pallas import tpu as pltpu

F32 = jnp.float32
BF16 = jnp.bfloat16

EPS = 1e-6
NEG = -1e30
ROPE_THETA = 10000.0

N_HEADS = 16
QK_NOPE_DIM = 128
QK_ROPE_DIM = 64
QK_DIM = QK_NOPE_DIM + QK_ROPE_DIM
V_HEAD_DIM = 128
CONV_WIDTH = 31

LANES = 128
SUBLANES = 8
VMEM_BYTES_V7X = 64 * 1024 * 1024
VMEM_LIMIT = VMEM_BYTES_V7X - 8 * 1024 * 1024

ROW_CHUNK = 16
ROW_GROUP = 16
GLU_COLS = 256
CONV_ROWS = 128
CONV_LANES = 128
HALO = 32
SHIFT_LANES = 128
MLP_TAIL_ROWS = 256
PROJ_ROWS = 512
VT_TILE = 256
ATTN_BLOCK = 512
ATTN_HEADS = 8


def _cparams(*sem):
    return pltpu.CompilerParams(dimension_semantics=sem, vmem_limit_bytes=VMEM_LIMIT)


def _dot(a, b):
    return jnp.dot(a, b, preferred_element_type=F32)


def _dot_nt(a, b):
    return lax.dot_general(a, b, (((1,), (1,)), ((), ())), preferred_element_type=F32)


def _sigmoid(x):
    return 1.0 / (1.0 + jnp.exp(-x))


def _row_loop(n_rows, body, row0=0, inline=False):
    span = ROW_CHUNK * ROW_GROUP
    if inline:
        for r0 in range(row0, row0 + n_rows, span):
            body([pl.ds(r0 + k * ROW_CHUNK, ROW_CHUNK) for k in range(ROW_GROUP)])
        return

    def step(i, carry):
        r0 = pl.multiple_of(row0 + i * span, span)
        body([pl.ds(r0 + k * ROW_CHUNK, ROW_CHUNK) for k in range(ROW_GROUP)])
        return carry
    lax.fori_loop(0, n_rows // span, step, 0)


def _inv_rms(x):
    return lax.rsqrt(jnp.mean(x * x, axis=-1, keepdims=True) + EPS)


def _ada_kernel(c_ref, w_ref, b_ref, o_ref):
    @pl.when(pl.program_id(1) == 0)
    def _():
        o_ref[0] = jnp.broadcast_to(b_ref[0], o_ref.shape[1:])

    c = c_ref[...]
    o_ref[0] += _dot((c * _sigmoid(c)).astype(BF16), w_ref[0].astype(BF16))


def _ada(c_pad, w, b, tk=256):
    nl, d, n = w.shape
    rows = c_pad.shape[0]
    return pl.pallas_call(
        _ada_kernel,
        grid=(nl, d // tk),
        in_specs=[
            pl.BlockSpec((rows, tk), lambda l, k: (0, k)),
            pl.BlockSpec((1, tk, n), lambda l, k: (l, k, 0)),
            pl.BlockSpec((1, 1, n), lambda l, k: (l, 0, 0)),
        ],
        out_specs=pl.BlockSpec((1, rows, n), lambda l, k: (l, 0, 0)),
        out_shape=jax.ShapeDtypeStruct((nl, rows, n), F32),
        compiler_params=_cparams("arbitrary", "arbitrary"),
        name="ada",
    )(c_pad, w, b.reshape(nl, 1, n))


def _norm_modulate_to(h_ref, g_ref, shift_ref, scale_ref, out_ref, n_rows, **loop_kw):
    mul = g_ref[...] * (1.0 + scale_ref[0])
    add = shift_ref[0]

    def body(chunks):
        for rows in chunks:
            x = h_ref[rows, :]
            out_ref[rows, :] = ((x * _inv_rms(x)) * mul + add).astype(BF16)
    _row_loop(n_rows, body, **loop_kw)


def _residual_gate_norm(h_ref, y_ref, gate_ref, g_ref, o_ref, n_rows, **loop_kw):
    mul = gate_ref[0] * g_ref[...]

    def body(chunks):
        ys = [y_ref[rows, :] for rows in chunks]
        scaled = [y * _inv_rms(y) for y in ys]
        for rows, s in zip(chunks, scaled):
            o_ref[rows, :] = h_ref[rows, :] + s * mul
    _row_loop(n_rows, body, **loop_kw)


def _vec_spec(d, seq_tiles):
    return pl.BlockSpec((1, 1, d), lambda m, *_: (m // seq_tiles, 0, 0))


def _row_spec(d):
    return pl.BlockSpec((1, d), lambda *_: (0, 0))


def _conv_in_kernel(h_ref, shift_ref, scale_ref, g_ref, wa_ref, wg_ref, ba_ref, bg_ref, u_ref, hn_ref):
    tm = h_ref.shape[0]

    def glu():
        hn = hn_ref[...]
        for c0 in range(0, wa_ref.shape[1], GLU_COLS):
            cols = slice(c0, c0 + GLU_COLS)
            a = _dot(hn, wa_ref[:, cols].astype(BF16)) + ba_ref[:, cols]
            g = _dot(hn, wg_ref[:, cols].astype(BF16)) + bg_ref[:, cols]
            u_ref[:, cols] = a * _sigmoid(g)

    @pl.when(pl.program_id(1) == 0)
    def _():
        _norm_modulate_to(h_ref, g_ref, shift_ref, scale_ref, hn_ref, tm, inline=True)
        glu()

    @pl.when(pl.program_id(1) != 0)
    def _():
        glu()


def _conv_in(h, shift, scale, g_pre, w_in, b_in, layer, seq, tm=1024, tn=512):
    t, d = h.shape
    nt = d // tn
    return pl.pallas_call(
        _conv_in_kernel,
        grid=(t // tm, nt),
        in_specs=[
            pl.BlockSpec((tm, d), lambda m, n: (m, 0)),
            _vec_spec(d, seq // tm), _vec_spec(d, seq // tm), _row_spec(d),
            pl.BlockSpec((None, d, tn), lambda m, n: (layer, 0, n)),
            pl.BlockSpec((None, d, tn), lambda m, n: (layer, 0, n + nt)),
            pl.BlockSpec((1, tn), lambda m, n: (0, n)),
            pl.BlockSpec((1, tn), lambda m, n: (0, n + nt)),
        ],
        out_specs=pl.BlockSpec((tm, tn), lambda m, n: (m, n)),
        out_shape=jax.ShapeDtypeStruct((t, d), F32),
        scratch_shapes=[pltpu.VMEM((tm, d), BF16)],
        compiler_params=_cparams("arbitrary", "arbitrary"),
        name="conv_in",
    )(h, shift, scale, g_pre, w_in, w_in, b_in, b_in)


def _conv_out_kernel(u_ref, halo_ref, dw_ref, dwb_ref, lng_ref, lnb_ref, wout_ref, bout_ref,
                     h_ref, gate_ref, gpost_ref, o_ref, sh_ref, cv_ref, a_ref, *, seq_tiles):
    tm, d = u_ref.shape
    first = (pl.program_id(0) % seq_tiles) == 0
    sh_ref[0, 0:HALO, :] = jnp.where(first, 0.0, halo_ref[...])
    sh_ref[0, HALO:, :] = u_ref[...]

    rows = tm + HALO - SUBLANES
    for c in range(d // SHIFT_LANES):
        lanes = slice(c * SHIFT_LANES, (c + 1) * SHIFT_LANES)
        x = sh_ref[0, :, lanes]
        for b in range(1, SUBLANES):
            sh_ref[b, 0:rows, lanes] = x[b:b + rows]

    base = HALO - (CONV_WIDTH - 1)
    groups = CONV_ROWS // SUBLANES
    taps_by_shift = {}
    for j in range(CONV_WIDTH):
        a, b = divmod(base + j, SUBLANES)
        taps_by_shift.setdefault(b, []).append((a, j))

    def conv_step(i, carry):
        r0 = pl.multiple_of(i * CONV_ROWS, CONV_ROWS)
        for c in range(d // CONV_LANES):
            lanes = slice(c * CONV_LANES, (c + 1) * CONV_LANES)
            accs = [jnp.broadcast_to(dwb_ref[:, lanes], (SUBLANES, CONV_LANES))] * groups
            for b, taps in taps_by_shift.items():
                ws = {j: dw_ref[j, :, lanes] for _, j in taps}
                tiles = [a for a, _ in taps]
                for k in range(min(tiles), max(tiles) + groups):
                    x = sh_ref[b, pl.ds(r0 + k * SUBLANES, SUBLANES), lanes]
                    for a, j in taps:
                        if 0 <= k - a < groups:
                            accs[k - a] = accs[k - a] + x * ws[j]
            for g in range(groups):
                cv_ref[pl.ds(r0 + g * SUBLANES, SUBLANES), lanes] = accs[g]
        return carry
    lax.fori_loop(0, tm // CONV_ROWS, conv_step, 0)

    lng = lng_ref[...]
    lnb = lnb_ref[...]

    def ln_body(chunks):
        for rows in chunks:
            x = cv_ref[rows, :]
            mu = jnp.mean(x, axis=-1, keepdims=True)
            xc = x - mu
            var = jnp.mean(xc * xc, axis=-1, keepdims=True)
            y = (xc * lax.rsqrt(var + EPS)) * lng + lnb
            a_ref[rows, :] = (y * _sigmoid(y)).astype(BF16)
    _row_loop(tm, ln_body, inline=True)

    cv_ref[...] = _dot(a_ref[...], wout_ref[...]) + bout_ref[...]
    _residual_gate_norm(h_ref, cv_ref, gate_ref, gpost_ref, o_ref, tm, inline=True)


def _conv_out(u, dw8, dw_b, ln_g, ln_b, w_out, b_out, h, gate, g_post, seq, tm=256):
    t, d = u.shape
    seq_tiles = seq // tm
    halo_blocks = tm // HALO
    return pl.pallas_call(
        functools.partial(_conv_out_kernel, seq_tiles=seq_tiles),
        grid=(t // tm,),
        in_specs=[
            pl.BlockSpec((tm, d), lambda m: (m, 0)),
            pl.BlockSpec((HALO, d), lambda m: (jnp.maximum(m * halo_blocks - 1, 0), 0)),
            pl.BlockSpec(dw8.shape, lambda m: (0, 0, 0), pipeline_mode=pl.Buffered(1)),
            _row_spec(d), _row_spec(d), _row_spec(d),
            pl.BlockSpec((d, d), lambda m: (0, 0), pipeline_mode=pl.Buffered(1)),
            _row_spec(d),
            pl.BlockSpec((tm, d), lambda m: (m, 0)),
            _vec_spec(d, seq_tiles), _row_spec(d),
        ],
        out_specs=pl.BlockSpec((tm, d), lambda m: (m, 0)),
        out_shape=jax.ShapeDtypeStruct((t, d), F32),
        scratch_shapes=[pltpu.VMEM((SUBLANES, tm + HALO, d), F32), pltpu.VMEM((tm, d), F32),
                        pltpu.VMEM((tm, d), BF16)],
        compiler_params=_cparams("arbitrary"),
        name="conv_out",
    )(u, u, dw8, dw_b, ln_g, ln_b, w_out, b_out, h, gate, g_post)


def _mlp_kernel(h_ref, shift_ref, scale_ref, gate_ref, gpre_ref, gpost_ref, wup_ref, wdown_ref,
                o_ref, hn_ref):
    tm = h_ref.shape[0]
    f = pl.program_id(1)
    last = pl.num_programs(1) - 1

    def hidden():
        up = jnp.maximum(_dot(hn_ref[...], wup_ref[...].astype(BF16)), 0.0)
        return (up * up).astype(BF16)

    @pl.when(f == 0)
    def _():
        _norm_modulate_to(h_ref, gpre_ref, shift_ref, scale_ref, hn_ref, tm, inline=True)
        o_ref[...] = _dot(hidden(), wdown_ref[...].astype(BF16))

    @pl.when((f != 0) & (f != last))
    def _():
        o_ref[...] += _dot(hidden(), wdown_ref[...].astype(BF16))

    @pl.when(f == last)
    def _():
        hid = hidden()
        wdown = wdown_ref[...].astype(BF16)
        for r0 in range(0, tm, MLP_TAIL_ROWS):
            rows = slice(r0, r0 + MLP_TAIL_ROWS)
            o_ref[rows, :] += _dot(hid[rows], wdown)
            _residual_gate_norm(h_ref, o_ref, gate_ref, gpost_ref, o_ref, MLP_TAIL_ROWS, row0=r0, inline=True)


def _mlp(h, shift, scale, gate, g_pre, g_post, w_up, w_down, layer, seq, tm=1024, tf=512):
    t, d = h.shape
    ff = w_up.shape[2]
    seq_tiles = seq // tm
    assert ff // tf >= 2, "the kernel's first and last ff steps must be distinct"
    return pl.pallas_call(
        _mlp_kernel,
        grid=(t // tm, ff // tf),
        in_specs=[
            pl.BlockSpec((tm, d), lambda m, f: (m, 0), pipeline_mode=pl.Buffered(1)),
            _vec_spec(d, seq_tiles), _vec_spec(d, seq_tiles), _vec_spec(d, seq_tiles),
            _row_spec(d), _row_spec(d),
            pl.BlockSpec((None, d, tf), lambda m, f: (layer, 0, f)),
            pl.BlockSpec((None, tf, d), lambda m, f: (layer, f, 0)),
        ],
        out_specs=pl.BlockSpec((tm, d), lambda m, f: (m, 0)),
        out_shape=jax.ShapeDtypeStruct((t, d), F32),
        scratch_shapes=[pltpu.VMEM((tm, d), BF16)],
        compiler_params=_cparams("arbitrary", "arbitrary"),
        name="mlp",
    )(h, shift, scale, gate, g_pre, g_post, w_up, w_down)


def _rope_t(x, cos, sin):
    half = QK_ROPE_DIM // 2
    x1, x2 = x[:half], x[half:]
    return x1 * cos - x2 * sin, x2 * cos + x1 * sin


def _proj_kernel(h_ref, pos_ref, inv_ref, kvshift_ref, kvscale_ref, shift_ref, scale_ref,
                 gkv_ref, gpre_ref, wdkv_ref, gckv_ref, wuk_ref, wuvt_ref, wdq_ref, gcq_ref, wuqt_ref,
                 qt_ref, k_ref, vt_ref, kvn_ref, hn_ref, *, q_scale):
    tm = h_ref.shape[0]
    kv_mul = gkv_ref[...] * (1.0 + kvscale_ref[0])
    kv_add = kvshift_ref[0]
    q_mul = gpre_ref[...] * (1.0 + scale_ref[0])
    q_add = shift_ref[0]

    def norm_body(chunks):
        for rows in chunks:
            x = h_ref[rows, :]
            xn = x * _inv_rms(x)
            kvn_ref[rows, :] = (xn * kv_mul + kv_add).astype(BF16)
            hn_ref[rows, :] = (xn * q_mul + q_add).astype(BF16)
    _row_loop(tm, norm_body, inline=True)

    ang = inv_ref[...] * pos_ref[...].astype(F32)
    cos = jnp.cos(ang)
    sin = jnp.sin(ang)

    r_kv = gckv_ref.shape[1]
    t1 = _dot(kvn_ref[...], wdkv_ref[...])
    ckv = t1[:, :r_kv]
    ckv = ((ckv * _inv_rms(ckv)) * gckv_ref[...]).astype(BF16)
    kr1, kr2 = _rope_t(t1[:, r_kv:].T[:QK_ROPE_DIM], cos, sin)
    k_rope = jnp.concatenate([kr1, kr2, jnp.zeros((LANES - QK_ROPE_DIM, tm), F32)], axis=0).T
    k_rope = k_rope[:, :QK_ROPE_DIM].astype(BF16)
    k_nope = _dot(ckv, wuk_ref[...]).astype(BF16)
    vt = _dot_nt(wuvt_ref[...], ckv).astype(BF16)

    cq = _dot(hn_ref[...], wdq_ref[...])
    cq = ((cq * _inv_rms(cq)) * gcq_ref[...]).astype(BF16)
    qt = _dot_nt(wuqt_ref[...], cq) * q_scale
    for hd in range(N_HEADS):
        r0 = hd * QK_DIM
        q1, q2 = _rope_t(qt[r0 + QK_NOPE_DIM:r0 + QK_DIM], cos, sin)
        qt_ref[0, hd, :QK_NOPE_DIM, :] = qt[r0:r0 + QK_NOPE_DIM].astype(BF16)
        qt_ref[0, hd, QK_NOPE_DIM:QK_NOPE_DIM + QK_ROPE_DIM // 2, :] = q1.astype(BF16)
        qt_ref[0, hd, QK_NOPE_DIM + QK_ROPE_DIM // 2:, :] = q2.astype(BF16)
        k_ref[0, hd, :, :QK_NOPE_DIM] = k_nope[:, hd * QK_NOPE_DIM:(hd + 1) * QK_NOPE_DIM]
        k_ref[0, hd, :, QK_NOPE_DIM:] = k_rope
        for i in range(vt_ref.shape[2]):
            vt_ref[0, hd, i] = vt[hd * V_HEAD_DIM:(hd + 1) * V_HEAD_DIM, i * VT_TILE:(i + 1) * VT_TILE]


def _proj(h, pos, inv_col, kv_shift, kv_scale, shift, scale, g_kv, g_pre, w_dkvkr, g_ckv, w_uk, w_uvt,
          w_dq, g_cq, w_uqt, q_scale, batch, seq):
    t, d = h.shape
    tm = PROJ_ROWS
    seq_tiles = seq // tm
    r_kv = g_ckv.shape[1]
    r_q = g_cq.shape[1]

    def full(a):
        return pl.BlockSpec(a.shape, lambda m: (0,) * a.ndim, pipeline_mode=pl.Buffered(1))

    return pl.pallas_call(
        functools.partial(_proj_kernel, q_scale=q_scale),
        grid=(t // tm,),
        in_specs=[
            pl.BlockSpec((tm, d), lambda m: (m, 0)),
            pl.BlockSpec((1, tm), lambda m: (0, m)),
            full(inv_col),
            _vec_spec(d, seq_tiles), _vec_spec(d, seq_tiles), _vec_spec(d, seq_tiles), _vec_spec(d, seq_tiles),
            _row_spec(d), _row_spec(d),
            full(w_dkvkr), _row_spec(r_kv), full(w_uk), full(w_uvt), full(w_dq), _row_spec(r_q), full(w_uqt),
        ],
        out_specs=[
            pl.BlockSpec((1, N_HEADS, QK_DIM, tm), lambda m: (m // seq_tiles, 0, 0, m % seq_tiles)),
            pl.BlockSpec((1, N_HEADS, tm, QK_DIM), lambda m: (m // seq_tiles, 0, m % seq_tiles, 0)),
            pl.BlockSpec((1, N_HEADS, tm // VT_TILE, V_HEAD_DIM, VT_TILE),
                         lambda m: (m // seq_tiles, 0, m % seq_tiles, 0, 0)),
        ],
        out_shape=[
            jax.ShapeDtypeStruct((batch, N_HEADS, QK_DIM, seq), BF16),
            jax.ShapeDtypeStruct((batch, N_HEADS, seq, QK_DIM), BF16),
            jax.ShapeDtypeStruct((batch, N_HEADS, seq // VT_TILE, V_HEAD_DIM, VT_TILE), BF16),
        ],
        scratch_shapes=[pltpu.VMEM((tm, d), BF16), pltpu.VMEM((tm, d), BF16)],
        compiler_params=_cparams("arbitrary"),
        name="proj",
    )(h, pos, inv_col, kv_shift, kv_scale, shift, scale, g_kv, g_pre, w_dkvkr, g_ckv, w_uk, w_uvt, w_dq, g_cq, w_uqt)


def _attn_kernel(qt_ref, k_ref, vt_ref, o_ref, sa_ref, sb_ref, acc_ref, st_ref, *, blk):
    qi = pl.program_id(2)
    heads = range(qt_ref.shape[1])
    half = vt_ref.shape[-1]
    assert blk == 2 * half

    def scores_to(s_ref, hd, tile):
        k = k_ref[0, hd, pl.ds(pl.multiple_of(tile * half, half), half), :]
        s = _dot(k, qt_ref[0, hd])
        s_ref[hd] = s
        return jnp.max(s, axis=0, keepdims=True)

    def update(s, mx, hd, tile, m, l):
        m_new = jnp.maximum(m, mx)
        alpha = jnp.exp2(m - m_new)
        p = jnp.exp2(s - m_new)
        l = alpha * l + jnp.sum(p, axis=0, keepdims=True)
        acc_ref[hd] = alpha * acc_ref[hd] + _dot(vt_ref[0, hd, tile], p.astype(BF16))
        return m_new, l

    def body(j, carry):
        out = []
        for hd, (mxa, m, l) in zip(heads, carry):
            mxb = scores_to(sb_ref, hd, 2 * j + 1)
            m, l = update(sa_ref[hd], mxa, hd, 2 * j, m, l)
            mxa = scores_to(sa_ref, hd, 2 * j + 2)
            m, l = update(sb_ref[hd], mxb, hd, 2 * j + 1, m, l)
            out.append((mxa, m, l))
        return tuple(out)

    acc_ref[...] = jnp.zeros_like(acc_ref)
    init = tuple((scores_to(sa_ref, hd, 0), jnp.full((1, blk), NEG, F32), jnp.zeros((1, blk), F32))
                 for hd in heads)
    carry = lax.fori_loop(0, qi, body, init)

    key = lax.broadcasted_iota(jnp.int32, (half, blk), 0)
    qry = lax.broadcasted_iota(jnp.int32, (half, blk), 1)
    key_sq = lax.broadcasted_iota(jnp.int32, (half, half), 0)
    qry_sq = lax.broadcasted_iota(jnp.int32, (half, half), 1)
    for hd, (_, m, l) in zip(heads, carry):
        kb = k_ref[0, hd, pl.ds(pl.multiple_of((2 * qi + 1) * half, half), half), :]
        sb = jnp.where(key_sq <= qry_sq, _dot(kb, qt_ref[0, hd, :, half:]), NEG)
        sa = jnp.where(key <= qry, sa_ref[hd], NEG)
        m, l = update(sa, jnp.max(sa, axis=0, keepdims=True), hd, 2 * qi, m, l)
        st_ref[0:1, :] = m
        st_ref[1:2, :] = l
        m_new = jnp.maximum(st_ref[0:1, half:], jnp.max(sb, axis=0, keepdims=True))
        alpha = jnp.exp2(st_ref[0:1, half:] - m_new)
        p = jnp.exp2(sb - m_new)
        st_ref[1:2, half:] = alpha * st_ref[1:2, half:] + jnp.sum(p, axis=0, keepdims=True)
        acc_ref[hd, :, half:] = alpha * acc_ref[hd, :, half:] + _dot(vt_ref[0, hd, 2 * qi + 1], p.astype(BF16))
        o_ref[0, :, hd * V_HEAD_DIM:(hd + 1) * V_HEAD_DIM] = (acc_ref[hd] / st_ref[1:2, :]).T.astype(BF16)


def _attention(qt, k, vt, blk=ATTN_BLOCK, nh=ATTN_HEADS):
    b, n_heads, s, _ = k.shape
    half = vt.shape[-1]
    return pl.pallas_call(
        functools.partial(_attn_kernel, blk=blk),
        grid=(b, n_heads // nh, s // blk),
        in_specs=[
            pl.BlockSpec((1, nh, QK_DIM, blk), lambda bi, hi, qi: (bi, hi, 0, qi)),
            pl.BlockSpec((1, nh, s, QK_DIM), lambda bi, hi, qi: (bi, hi, 0, 0)),
            pl.BlockSpec((1, nh) + vt.shape[2:], lambda bi, hi, qi: (bi, hi, 0, 0, 0)),
        ],
        out_specs=pl.BlockSpec((1, blk, nh * V_HEAD_DIM), lambda bi, hi, qi: (bi, qi, hi)),
        out_shape=jax.ShapeDtypeStruct((b, s, n_heads * V_HEAD_DIM), BF16),
        scratch_shapes=[pltpu.VMEM((nh, half, blk), F32), pltpu.VMEM((nh, half, blk), F32),
                        pltpu.VMEM((nh, V_HEAD_DIM, blk), F32), pltpu.VMEM((SUBLANES, blk), F32)],
        compiler_params=_cparams("arbitrary", "arbitrary", "arbitrary"),
        name="attn",
    )(qt, k, vt)


def _attn_out_kernel(a_ref, wo_ref, h_ref, gate_ref, gpost_ref, o_ref, wbf_ref):
    tm = h_ref.shape[0]

    @pl.when(pl.program_id(0) == 0)
    def _():
        wbf_ref[...] = wo_ref[...].astype(BF16)

    o_ref[...] = _dot(a_ref[...], wbf_ref[...])
    _residual_gate_norm(h_ref, o_ref, gate_ref, gpost_ref, o_ref, tm, inline=True)


def _attn_out(a, w_o, h, gate, g_post, seq, tm=256):
    t, d = h.shape
    seq_tiles = seq // tm
    return pl.pallas_call(
        _attn_out_kernel,
        grid=(t // tm,),
        in_specs=[
            pl.BlockSpec((tm, a.shape[1]), lambda m: (m, 0)),
            pl.BlockSpec(w_o.shape, lambda m: (0, 0), pipeline_mode=pl.Buffered(1)),
            pl.BlockSpec((tm, d), lambda m: (m, 0)),
            _vec_spec(d, seq_tiles), _row_spec(d),
        ],
        out_specs=pl.BlockSpec((tm, d), lambda m: (m, 0)),
        out_shape=jax.ShapeDtypeStruct((t, d), F32),
        scratch_shapes=[pltpu.VMEM(w_o.shape, BF16)],
        compiler_params=_cparams("arbitrary"),
        name="attn_out",
    )(a, w_o, h, gate, g_post)


def kernel(x, c, positions, w_ada_mix, b_ada_mix, w_ada_mlp, b_ada_mlp, g_pre_mix, g_post_mix, g_pre_mlp,
           g_post_mlp, conv_w_in, conv_b_in, conv_dw, conv_dw_b, conv_ln_g, conv_ln_b, conv_w_out, conv_b_out,
           w_ada_kv, b_ada_kv, g_kv, w_dkv, g_ckv, w_kr, w_uk, w_uv, w_dq, g_cq, w_uq, w_o, mlp_w_up,
           mlp_w_down):
    batch, seq, d = x.shape
    depth = w_ada_mix.shape[0]
    n_conv = conv_w_in.shape[0]
    t = batch * seq
    if depth - n_conv != 1:
        raise NotImplementedError("exactly one MLA layer reads the shared K/V in this trunk")

    c_pad = jnp.pad(c, ((0, -batch % SUBLANES), (0, 0)))

    def split(m, n):
        return [m[:batch, None, i * d:(i + 1) * d] for i in range(n)]

    ada_mix = _ada(c_pad, w_ada_mix, b_ada_mix)
    ada_mlp = _ada(c_pad, w_ada_mlp, b_ada_mlp)
    kv_shift, kv_scale = split(_ada(c_pad, w_ada_kv[None], b_ada_kv[None])[0], 2)

    def row(v):
        return v.reshape(1, -1)

    inv = 1.0 / (ROPE_THETA ** (jnp.arange(0, QK_ROPE_DIM, 2, dtype=F32) / QK_ROPE_DIM))
    inv_col = inv.reshape(-1, 1)
    pos = positions.reshape(1, t)
    q_scale = QK_DIM ** -0.5 * math.log2(math.e)

    h = x.reshape(t, d)
    for l in range(depth):
        shift, scale, gate = split(ada_mix[l], 3)
        if l < n_conv:
            u = _conv_in(h, shift, scale, row(g_pre_mix[l]), conv_w_in, row(conv_b_in[l]), l, seq)
            dw8 = jnp.broadcast_to(conv_dw[l][:, None, :], (CONV_WIDTH, SUBLANES, d))
            h = _conv_out(u, dw8, row(conv_dw_b[l]), row(conv_ln_g[l]), row(conv_ln_b[l]),
                          conv_w_out[l].astype(BF16), row(conv_b_out[l]), h, gate, row(g_post_mix[l]), seq)
        else:
            j = l - n_conv
            w_dkvkr = jnp.concatenate(
                [w_dkv, w_kr, jnp.zeros((d, LANES - QK_ROPE_DIM), F32)], axis=1).astype(BF16)
            qt, k, vt = _proj(h, pos, inv_col, kv_shift, kv_scale, shift, scale, row(g_kv), row(g_pre_mix[l]),
                              w_dkvkr, row(g_ckv), w_uk.astype(BF16), w_uv.T.astype(BF16),
                              w_dq[j].astype(BF16), row(g_cq[j]), w_uq[j].T.astype(BF16), q_scale, batch, seq)
            a = _attention(qt, k, vt)
            h = _attn_out(a.reshape(t, -1), w_o[j], h, gate, row(g_post_mix[l]), seq)
        shift, scale, gate = split(ada_mlp[l], 3)
        h = _mlp(h, shift, scale, gate, row(g_pre_mlp[l]), row(g_post_mlp[l]), mlp_w_up, mlp_w_down, l, seq)
    return h.reshape(batch, seq, d)
```

```python
import functools
import math

import jax
import jax.numpy as jnp
from jax import lax
from jax.experimental import pallas as pl
from jax.experimental.pallas import tpu as pltpu

F32 = jnp.float32
BF16 = jnp.bfloat16

EPS = 1e-6
NEG = -1e30
ROPE_THETA = 10000.0

N_HEADS = 16
QK_NOPE_DIM = 128
QK_ROPE_DIM = 64
QK_DIM = QK_NOPE_DIM + QK_ROPE_DIM
V_HEAD_DIM = 128
CONV_WIDTH = 31

LANES = 128
SUBLANES = 8
VMEM_BYTES_V7X = 64 * 1024 * 1024
VMEM_LIMIT = VMEM_BYTES_V7X - 8 * 1024 * 1024

ROW_CHUNK = 16
ROW_GROUP = 16
GLU_COLS = 256
CONV_ROWS = 128
CONV_LANES = 128
HALO = 32
SHIFT_TILES = 5
MLP_TAIL_ROWS = 256
PROJ_ROWS = 512
VT_TILE = 256
ATTN_BLOCK = 512
ATTN_HEADS = 8


def _cparams(*sem):
    return pltpu.CompilerParams(dimension_semantics=sem, vmem_limit_bytes=VMEM_LIMIT)


def _dot(a, b):
    return jnp.dot(a, b, preferred_element_type=F32)


def _dot_nt(a, b):
    return lax.dot_general(a, b, (((1,), (1,)), ((), ())), preferred_element_type=F32)


def _sigmoid(x):
    return 1.0 / (1.0 + jnp.exp(-x))


def _row_loop(n_rows, body, row0=0, inline=False):
    span = ROW_CHUNK * ROW_GROUP
    if inline:
        for r0 in range(row0, row0 + n_rows, span):
            body([pl.ds(r0 + k * ROW_CHUNK, ROW_CHUNK) for k in range(ROW_GROUP)])
        return

    def step(i, carry):
        r0 = pl.multiple_of(row0 + i * span, span)
        body([pl.ds(r0 + k * ROW_CHUNK, ROW_CHUNK) for k in range(ROW_GROUP)])
        return carry
    lax.fori_loop(0, n_rows // span, step, 0)


def _inv_rms(x):
    return lax.rsqrt(jnp.mean(x * x, axis=-1, keepdims=True) + EPS)


def _ada_kernel(c_ref, w_ref, b_ref, o_ref):
    @pl.when(pl.program_id(1) == 0)
    def _():
        o_ref[0] = jnp.broadcast_to(b_ref[0], o_ref.shape[1:])

    c = c_ref[...]
    o_ref[0] += _dot((c * _sigmoid(c)).astype(BF16), w_ref[0].astype(BF16))


def _ada(c_pad, w, b, tk=512):
    nl, d, n = w.shape
    rows = c_pad.shape[0]
    return pl.pallas_call(
        _ada_kernel,
        grid=(nl, d // tk),
        in_specs=[
            pl.BlockSpec((rows, tk), lambda l, k: (0, k)),
            pl.BlockSpec((1, tk, n), lambda l, k: (l, k, 0)),
            pl.BlockSpec((1, 1, n), lambda l, k: (l, 0, 0)),
        ],
        out_specs=pl.BlockSpec((1, rows, n), lambda l, k: (l, 0, 0)),
        out_shape=jax.ShapeDtypeStruct((nl, rows, n), F32),
        compiler_params=_cparams("arbitrary", "arbitrary"),
        name="ada",
    )(c_pad, w, b.reshape(nl, 1, n))


def _norm_modulate_to(h_ref, g_ref, shift_ref, scale_ref, out_ref, n_rows, **loop_kw):
    mul = g_ref[...] * (1.0 + scale_ref[0])
    add = shift_ref[0]

    def body(chunks):
        for rows in chunks:
            x = h_ref[rows, :]
            out_ref[rows, :] = ((x * _inv_rms(x)) * mul + add).astype(BF16)
    _row_loop(n_rows, body, **loop_kw)


def _residual_gate_norm(h_ref, y_ref, gate_ref, g_ref, o_ref, n_rows, **loop_kw):
    mul = gate_ref[0] * g_ref[...]

    def body(chunks):
        ys = [y_ref[rows, :] for rows in chunks]
        scaled = [y * _inv_rms(y) for y in ys]
        for rows, s in zip(chunks, scaled):
            o_ref[rows, :] = h_ref[rows, :] + s * mul
    _row_loop(n_rows, body, **loop_kw)


def _vec_spec(d, seq_tiles):
    return pl.BlockSpec((1, 1, d), lambda m, *_: (m // seq_tiles, 0, 0))


def _row_spec(d):
    return pl.BlockSpec((1, d), lambda *_: (0, 0))


def _conv_in_kernel(h_ref, shift_ref, scale_ref, g_ref, wa_ref, wg_ref, ba_ref, bg_ref, u_ref, hn_ref):
    tm = h_ref.shape[0]

    def glu():
        hn = hn_ref[...]
        for c0 in range(0, wa_ref.shape[1], GLU_COLS):
            cols = slice(c0, c0 + GLU_COLS)
            a = _dot(hn, wa_ref[:, cols].astype(BF16)) + ba_ref[:, cols]
            g = _dot(hn, wg_ref[:, cols].astype(BF16)) + bg_ref[:, cols]
            u_ref[:, cols] = a * _sigmoid(g)

    @pl.when(pl.program_id(1) == 0)
    def _():
        _norm_modulate_to(h_ref, g_ref, shift_ref, scale_ref, hn_ref, tm, inline=True)
        glu()

    @pl.when(pl.program_id(1) != 0)
    def _():
        glu()


def _conv_in(h, shift, scale, g_pre, w_in, b_in, layer, seq, tm=1024, tn=512):
    t, d = h.shape
    nt = d // tn
    return pl.pallas_call(
        _conv_in_kernel,
        grid=(t // tm, nt),
        in_specs=[
            pl.BlockSpec((tm, d), lambda m, n: (m, 0)),
            _vec_spec(d, seq // tm), _vec_spec(d, seq // tm), _row_spec(d),
            pl.BlockSpec((None, d, tn), lambda m, n: (layer, 0, n)),
            pl.BlockSpec((None, d, tn), lambda m, n: (layer, 0, n + nt)),
            pl.BlockSpec((1, tn), lambda m, n: (0, n)),
            pl.BlockSpec((1, tn), lambda m, n: (0, n + nt)),
        ],
        out_specs=pl.BlockSpec((tm, tn), lambda m, n: (m, n)),
        out_shape=jax.ShapeDtypeStruct((t, d), F32),
        scratch_shapes=[pltpu.VMEM((tm, d), BF16)],
        compiler_params=_cparams("arbitrary", "arbitrary"),
        name="conv_in",
    )(h, shift, scale, g_pre, w_in, w_in, b_in, b_in)


def _conv_out_kernel(u_ref, halo_ref, dw_ref, dwb_ref, lng_ref, lnb_ref, wout_ref, bout_ref,
                     h_ref, gate_ref, gpost_ref, o_ref, sh_ref, cv_ref, a_ref, *, seq_tiles):
    tm, d = u_ref.shape
    first = (pl.program_id(0) % seq_tiles) == 0
    sh_ref[0, 0:HALO, :] = jnp.where(first, 0.0, halo_ref[...])
    sh_ref[0, HALO:, :] = u_ref[...]

    span = SHIFT_TILES * SUBLANES

    def shift_step(i, carry):
        r0 = pl.multiple_of(i * span, span)
        x = sh_ref[0, pl.ds(r0, span + SUBLANES), :]
        for b in range(1, SUBLANES):
            sh_ref[b, pl.ds(r0, span), :] = x[b:b + span]
        return carry
    lax.fori_loop(0, (tm + HALO - SUBLANES) // span, shift_step, 0)

    base = HALO - (CONV_WIDTH - 1)
    groups = CONV_ROWS // SUBLANES
    taps_by_shift = {}
    for j in range(CONV_WIDTH):
        a, b = divmod(base + j, SUBLANES)
        taps_by_shift.setdefault(b, []).append((a, j))

    def conv_step(i, carry):
        r0 = pl.multiple_of(i * CONV_ROWS, CONV_ROWS)
        for c in range(d // CONV_LANES):
            lanes = slice(c * CONV_LANES, (c + 1) * CONV_LANES)
            accs = [jnp.broadcast_to(dwb_ref[:, lanes], (SUBLANES, CONV_LANES))] * groups
            for b, taps in taps_by_shift.items():
                ws = {j: dw_ref[j, :, lanes] for _, j in taps}
                tiles = [a for a, _ in taps]
                for k in range(min(tiles), max(tiles) + groups):
                    x = sh_ref[b, pl.ds(r0 + k * SUBLANES, SUBLANES), lanes]
                    for a, j in taps:
                        if 0 <= k - a < groups:
                            accs[k - a] = accs[k - a] + x * ws[j]
            for g in range(groups):
                cv_ref[pl.ds(r0 + g * SUBLANES, SUBLANES), lanes] = accs[g]
        return carry
    lax.fori_loop(0, tm // CONV_ROWS, conv_step, 0)

    lng = lng_ref[...]
    lnb = lnb_ref[...]

    def ln_body(chunks):
        for rows in chunks:
            x = cv_ref[rows, :]
            mu = jnp.mean(x, axis=-1, keepdims=True)
            xc = x - mu
            var = jnp.mean(xc * xc, axis=-1, keepdims=True)
            y = (xc * lax.rsqrt(var + EPS)) * lng + lnb
            a_ref[rows, :] = (y * _sigmoid(y)).astype(BF16)
    _row_loop(tm, ln_body, inline=True)

    cv_ref[...] = _dot(a_ref[...], wout_ref[...]) + bout_ref[...]
    _residual_gate_norm(h_ref, cv_ref, gate_ref, gpost_ref, o_ref, tm, inline=True)


def _conv_out(u, dw8, dw_b, ln_g, ln_b, w_out, b_out, h, gate, g_post, seq, tm=256):
    t, d = u.shape
    seq_tiles = seq // tm
    halo_blocks = tm // HALO
    return pl.pallas_call(
        functools.partial(_conv_out_kernel, seq_tiles=seq_tiles),
        grid=(t // tm,),
        in_specs=[
            pl.BlockSpec((tm, d), lambda m: (m, 0)),
            pl.BlockSpec((HALO, d), lambda m: (jnp.maximum(m * halo_blocks - 1, 0), 0)),
            pl.BlockSpec(dw8.shape, lambda m: (0, 0, 0), pipeline_mode=pl.Buffered(1)),
            _row_spec(d), _row_spec(d), _row_spec(d),
            pl.BlockSpec((d, d), lambda m: (0, 0), pipeline_mode=pl.Buffered(1)),
            _row_spec(d),
            pl.BlockSpec((tm, d), lambda m: (m, 0)),
            _vec_spec(d, seq_tiles), _row_spec(d),
        ],
        out_specs=pl.BlockSpec((tm, d), lambda m: (m, 0)),
        out_shape=jax.ShapeDtypeStruct((t, d), F32),
        scratch_shapes=[pltpu.VMEM((SUBLANES, tm + HALO, d), F32), pltpu.VMEM((tm, d), F32),
                        pltpu.VMEM((tm, d), BF16)],
        compiler_params=_cparams("arbitrary"),
        name="conv_out",
    )(u, u, dw8, dw_b, ln_g, ln_b, w_out, b_out, h, gate, g_post)


def _mlp_kernel(h_ref, shift_ref, scale_ref, gate_ref, gpre_ref, gpost_ref, wup_ref, wdown_ref,
                o_ref, hn_ref):
    tm = h_ref.shape[0]
    f = pl.program_id(1)
    last = pl.num_programs(1) - 1

    def hidden():
        up = jnp.maximum(_dot(hn_ref[...], wup_ref[...].astype(BF16)), 0.0)
        return (up * up).astype(BF16)

    @pl.when(f == 0)
    def _():
        _norm_modulate_to(h_ref, gpre_ref, shift_ref, scale_ref, hn_ref, tm, inline=True)
        o_ref[...] = _dot(hidden(), wdown_ref[...].astype(BF16))

    @pl.when((f != 0) & (f != last))
    def _():
        o_ref[...] += _dot(hidden(), wdown_ref[...].astype(BF16))

    @pl.when(f == last)
    def _():
        hid = hidden()
        wdown = wdown_ref[...].astype(BF16)
        for r0 in range(0, tm, MLP_TAIL_ROWS):
            rows = slice(r0, r0 + MLP_TAIL_ROWS)
            o_ref[rows, :] += _dot(hid[rows], wdown)
            _residual_gate_norm(h_ref, o_ref, gate_ref, gpost_ref, o_ref, MLP_TAIL_ROWS, row0=r0, inline=True)


def _mlp(h, shift, scale, gate, g_pre, g_post, w_up, w_down, layer, seq, tm=1024, tf=512):
    t, d = h.shape
    ff = w_up.shape[2]
    seq_tiles = seq // tm
    assert ff // tf >= 2, "the kernel's first and last ff steps must be distinct"
    return pl.pallas_call(
        _mlp_kernel,
        grid=(t // tm, ff // tf),
        in_specs=[
            pl.BlockSpec((tm, d), lambda m, f: (m, 0), pipeline_mode=pl.Buffered(1)),
            _vec_spec(d, seq_tiles), _vec_spec(d, seq_tiles), _vec_spec(d, seq_tiles),
            _row_spec(d), _row_spec(d),
            pl.BlockSpec((None, d, tf), lambda m, f: (layer, 0, f)),
            pl.BlockSpec((None, tf, d), lambda m, f: (layer, f, 0)),
        ],
        out_specs=pl.BlockSpec((tm, d), lambda m, f: (m, 0)),
        out_shape=jax.ShapeDtypeStruct((t, d), F32),
        scratch_shapes=[pltpu.VMEM((tm, d), BF16)],
        compiler_params=_cparams("arbitrary", "arbitrary"),
        name="mlp",
    )(h, shift, scale, gate, g_pre, g_post, w_up, w_down)


def _rope_t(x, cos, sin):
    half = QK_ROPE_DIM // 2
    x1, x2 = x[:half], x[half:]
    return x1 * cos - x2 * sin, x2 * cos + x1 * sin


def _proj_kernel(h_ref, pos_ref, inv_ref, kvshift_ref, kvscale_ref, shift_ref, scale_ref,
                 gkv_ref, gpre_ref, wdkv_ref, gckv_ref, wuk_ref, wuvt_ref, wdq_ref, gcq_ref, wuqt_ref,
                 qt_ref, k_ref, vt_ref, kvn_ref, hn_ref, *, q_scale):
    tm = h_ref.shape[0]
    kv_mul = gkv_ref[...] * (1.0 + kvscale_ref[0])
    kv_add = kvshift_ref[0]
    q_mul = gpre_ref[...] * (1.0 + scale_ref[0])
    q_add = shift_ref[0]

    def norm_body(chunks):
        for rows in chunks:
            x = h_ref[rows, :]
            xn = x * _inv_rms(x)
            kvn_ref[rows, :] = (xn * kv_mul + kv_add).astype(BF16)
            hn_ref[rows, :] = (xn * q_mul + q_add).astype(BF16)
    _row_loop(tm, norm_body, inline=True)

    ang = inv_ref[...] * pos_ref[...].astype(F32)
    cos = jnp.cos(ang)
    sin = jnp.sin(ang)

    r_kv = gckv_ref.shape[1]
    t1 = _dot(kvn_ref[...], wdkv_ref[...])
    ckv = t1[:, :r_kv]
    ckv = ((ckv * _inv_rms(ckv)) * gckv_ref[...]).astype(BF16)
    kr1, kr2 = _rope_t(t1[:, r_kv:].T[:QK_ROPE_DIM], cos, sin)
    k_rope = jnp.concatenate([kr1, kr2, jnp.zeros((LANES - QK_ROPE_DIM, tm), F32)], axis=0).T
    k_rope = k_rope[:, :QK_ROPE_DIM].astype(BF16)
    k_nope = _dot(ckv, wuk_ref[...]).astype(BF16)
    vt = _dot_nt(wuvt_ref[...], ckv).astype(BF16)

    cq = _dot(hn_ref[...], wdq_ref[...])
    cq = ((cq * _inv_rms(cq)) * gcq_ref[...]).astype(BF16)
    qt = _dot_nt(wuqt_ref[...], cq) * q_scale
    for hd in range(N_HEADS):
        r0 = hd * QK_DIM
        q1, q2 = _rope_t(qt[r0 + QK_NOPE_DIM:r0 + QK_DIM], cos, sin)
        qt_ref[0, hd, :QK_NOPE_DIM, :] = qt[r0:r0 + QK_NOPE_DIM].astype(BF16)
        qt_ref[0, hd, QK_NOPE_DIM:QK_NOPE_DIM + QK_ROPE_DIM // 2, :] = q1.astype(BF16)
        qt_ref[0, hd, QK_NOPE_DIM + QK_ROPE_DIM // 2:, :] = q2.astype(BF16)
        k_ref[0, hd, :, :QK_NOPE_DIM] = k_nope[:, hd * QK_NOPE_DIM:(hd + 1) * QK_NOPE_DIM]
        k_ref[0, hd, :, QK_NOPE_DIM:] = k_rope
        for i in range(vt_ref.shape[2]):
            vt_ref[0, hd, i] = vt[hd * V_HEAD_DIM:(hd + 1) * V_HEAD_DIM, i * VT_TILE:(i + 1) * VT_TILE]


def _proj(h, pos, inv_col, kv_shift, kv_scale, shift, scale, g_kv, g_pre, w_dkvkr, g_ckv, w_uk, w_uvt,
          w_dq, g_cq, w_uqt, q_scale, batch, seq):
    t, d = h.shape
    tm = PROJ_ROWS
    seq_tiles = seq // tm
    r_kv = g_ckv.shape[1]
    r_q = g_cq.shape[1]

    def full(a):
        return pl.BlockSpec(a.shape, lambda m: (0,) * a.ndim, pipeline_mode=pl.Buffered(1))

    return pl.pallas_call(
        functools.partial(_proj_kernel, q_scale=q_scale),
        grid=(t // tm,),
        in_specs=[
            pl.BlockSpec((tm, d), lambda m: (m, 0)),
            pl.BlockSpec((1, tm), lambda m: (0, m)),
            full(inv_col),
            _vec_spec(d, seq_tiles), _vec_spec(d, seq_tiles), _vec_spec(d, seq_tiles), _vec_spec(d, seq_tiles),
            _row_spec(d), _row_spec(d),
            full(w_dkvkr), _row_spec(r_kv), full(w_uk), full(w_uvt), full(w_dq), _row_spec(r_q), full(w_uqt),
        ],
        out_specs=[
            pl.BlockSpec((1, N_HEADS, QK_DIM, tm), lambda m: (m // seq_tiles, 0, 0, m % seq_tiles)),
            pl.BlockSpec((1, N_HEADS, tm, QK_DIM), lambda m: (m // seq_tiles, 0, m % seq_tiles, 0)),
            pl.BlockSpec((1, N_HEADS, tm // VT_TILE, V_HEAD_DIM, VT_TILE),
                         lambda m: (m // seq_tiles, 0, m % seq_tiles, 0, 0)),
        ],
        out_shape=[
            jax.ShapeDtypeStruct((batch, N_HEADS, QK_DIM, seq), BF16),
            jax.ShapeDtypeStruct((batch, N_HEADS, seq, QK_DIM), BF16),
            jax.ShapeDtypeStruct((batch, N_HEADS, seq // VT_TILE, V_HEAD_DIM, VT_TILE), BF16),
        ],
        scratch_shapes=[pltpu.VMEM((tm, d), BF16), pltpu.VMEM((tm, d), BF16)],
        compiler_params=_cparams("arbitrary"),
        name="proj",
    )(h, pos, inv_col, kv_shift, kv_scale, shift, scale, g_kv, g_pre, w_dkvkr, g_ckv, w_uk, w_uvt, w_dq, g_cq, w_uqt)


def _attn_kernel(qt_ref, k_ref, vt_ref, o_ref, sa_ref, sb_ref, acc_ref, st_ref, *, blk):
    qi = pl.program_id(2)
    heads = range(qt_ref.shape[1])
    half = vt_ref.shape[-1]
    assert blk == 2 * half

    def scores_to(s_ref, hd, tile):
        k = k_ref[0, hd, pl.ds(pl.multiple_of(tile * half, half), half), :]
        s = _dot(k, qt_ref[0, hd])
        s_ref[hd] = s
        return jnp.max(s, axis=0, keepdims=True)

    def update(s, mx, hd, tile, m, l):
        m_new = jnp.maximum(m, mx)
        alpha = jnp.exp2(m - m_new)
        p = jnp.exp2(s - m_new)
        l = alpha * l + jnp.sum(p, axis=0, keepdims=True)
        acc_ref[hd] = alpha * acc_ref[hd] + _dot(vt_ref[0, hd, tile], p.astype(BF16))
        return m_new, l

    def body(j, carry):
        out = []
        for hd, (mxa, m, l) in zip(heads, carry):
            mxb = scores_to(sb_ref, hd, 2 * j + 1)
            m, l = update(sa_ref[hd], mxa, hd, 2 * j, m, l)
            mxa = scores_to(sa_ref, hd, 2 * j + 2)
            m, l = update(sb_ref[hd], mxb, hd, 2 * j + 1, m, l)
            out.append((mxa, m, l))
        return tuple(out)

    acc_ref[...] = jnp.zeros_like(acc_ref)
    init = tuple((scores_to(sa_ref, hd, 0), jnp.full((1, blk), NEG, F32), jnp.zeros((1, blk), F32))
                 for hd in heads)
    carry = lax.fori_loop(0, qi, body, init)

    key = lax.broadcasted_iota(jnp.int32, (half, blk), 0)
    qry = lax.broadcasted_iota(jnp.int32, (half, blk), 1)
    key_sq = lax.broadcasted_iota(jnp.int32, (half, half), 0)
    qry_sq = lax.broadcasted_iota(jnp.int32, (half, half), 1)
    for hd, (_, m, l) in zip(heads, carry):
        kb = k_ref[0, hd, pl.ds(pl.multiple_of((2 * qi + 1) * half, half), half), :]
        sb = jnp.where(key_sq <= qry_sq, _dot(kb, qt_ref[0, hd, :, half:]), NEG)
        sa = jnp.where(key <= qry, sa_ref[hd], NEG)
        m, l = update(sa, jnp.max(sa, axis=0, keepdims=True), hd, 2 * qi, m, l)
        st_ref[0:1, :] = m
        st_ref[1:2, :] = l
        m_new = jnp.maximum(st_ref[0:1, half:], jnp.max(sb, axis=0, keepdims=True))
        alpha = jnp.exp2(st_ref[0:1, half:] - m_new)
        p = jnp.exp2(sb - m_new)
        st_ref[1:2, half:] = alpha * st_ref[1:2, half:] + jnp.sum(p, axis=0, keepdims=True)
        acc_ref[hd, :, half:] = alpha * acc_ref[hd, :, half:] + _dot(vt_ref[0, hd, 2 * qi + 1], p.astype(BF16))
        o_ref[0, :, hd * V_HEAD_DIM:(hd + 1) * V_HEAD_DIM] = (acc_ref[hd] / st_ref[1:2, :]).T.astype(BF16)


def _attention(qt, k, vt, blk=ATTN_BLOCK, nh=ATTN_HEADS):
    b, n_heads, s, _ = k.shape
    half = vt.shape[-1]
    return pl.pallas_call(
        functools.partial(_attn_kernel, blk=blk),
        grid=(b, n_heads // nh, s // blk),
        in_specs=[
            pl.BlockSpec((1, nh, QK_DIM, blk), lambda bi, hi, qi: (bi, hi, 0, qi)),
            pl.BlockSpec((1, nh, s, QK_DIM), lambda bi, hi, qi: (bi, hi, 0, 0)),
            pl.BlockSpec((1, nh) + vt.shape[2:], lambda bi, hi, qi: (bi, hi, 0, 0, 0)),
        ],
        out_specs=pl.BlockSpec((1, blk, nh * V_HEAD_DIM), lambda bi, hi, qi: (bi, qi, hi)),
        out_shape=jax.ShapeDtypeStruct((b, s, n_heads * V_HEAD_DIM), BF16),
        scratch_shapes=[pltpu.VMEM((nh, half, blk), F32), pltpu.VMEM((nh, half, blk), F32),
                        pltpu.VMEM((nh, V_HEAD_DIM, blk), F32), pltpu.VMEM((SUBLANES, blk), F32)],
        compiler_params=_cparams("arbitrary", "arbitrary", "arbitrary"),
        name="attn",
    )(qt, k, vt)


def _attn_out_kernel(a_ref, wo_ref, h_ref, gate_ref, gpost_ref, o_ref, wbf_ref):
    tm = h_ref.shape[0]

    @pl.when(pl.program_id(0) == 0)
    def _():
        wbf_ref[...] = wo_ref[...].astype(BF16)

    o_ref[...] = _dot(a_ref[...], wbf_ref[...])
    _residual_gate_norm(h_ref, o_ref, gate_ref, gpost_ref, o_ref, tm, inline=True)


def _attn_out(a, w_o, h, gate, g_post, seq, tm=512):
    t, d = h.shape
    seq_tiles = seq // tm
    return pl.pallas_call(
        _attn_out_kernel,
        grid=(t // tm,),
        in_specs=[
            pl.BlockSpec((tm, a.shape[1]), lambda m: (m, 0)),
            pl.BlockSpec(w_o.shape, lambda m: (0, 0), pipeline_mode=pl.Buffered(1)),
            pl.BlockSpec((tm, d), lambda m: (m, 0)),
            _vec_spec(d, seq_tiles), _row_spec(d),
        ],
        out_specs=pl.BlockSpec((tm, d), lambda m: (m, 0)),
        out_shape=jax.ShapeDtypeStruct((t, d), F32),
        scratch_shapes=[pltpu.VMEM(w_o.shape, BF16)],
        compiler_params=_cparams("arbitrary"),
        name="attn_out",
    )(a, w_o, h, gate, g_post)


def kernel(x, c, positions, w_ada_mix, b_ada_mix, w_ada_mlp, b_ada_mlp, g_pre_mix, g_post_mix, g_pre_mlp,
           g_post_mlp, conv_w_in, conv_b_in, conv_dw, conv_dw_b, conv_ln_g, conv_ln_b, conv_w_out, conv_b_out,
           w_ada_kv, b_ada_kv, g_kv, w_dkv, g_ckv, w_kr, w_uk, w_uv, w_dq, g_cq, w_uq, w_o, mlp_w_up,
           mlp_w_down):
    batch, seq, d = x.shape
    depth = w_ada_mix.shape[0]
    n_conv = conv_w_in.shape[0]
    t = batch * seq
    if depth - n_conv != 1:
        raise NotImplementedError("exactly one MLA layer reads the shared K/V in this trunk")

    c_pad = jnp.pad(c, ((0, -batch % SUBLANES), (0, 0)))

    def split(m, n):
        return [m[:batch, None, i * d:(i + 1) * d] for i in range(n)]

    ada_mix = _ada(c_pad, w_ada_mix, b_ada_mix)
    ada_mlp = _ada(c_pad, w_ada_mlp, b_ada_mlp)
    kv_shift, kv_scale = split(_ada(c_pad, w_ada_kv[None], b_ada_kv[None])[0], 2)

    def row(v):
        return v.reshape(1, -1)

    inv = 1.0 / (ROPE_THETA ** (jnp.arange(0, QK_ROPE_DIM, 2, dtype=F32) / QK_ROPE_DIM))
    inv_col = inv.reshape(-1, 1)
    pos = positions.reshape(1, t)
    q_scale = QK_DIM ** -0.5 * math.log2(math.e)

    h = x.reshape(t, d)
    for l in range(depth):
        shift, scale, gate = split(ada_mix[l], 3)
        if l < n_conv:
            u = _conv_in(h, shift, scale, row(g_pre_mix[l]), conv_w_in, row(conv_b_in[l]), l, seq)
            dw8 = jnp.broadcast_to(conv_dw[l][:, None, :], (CONV_WIDTH, SUBLANES, d))
            h = _conv_out(u, dw8, row(conv_dw_b[l]), row(conv_ln_g[l]), row(conv_ln_b[l]),
                          conv_w_out[l].astype(BF16), row(conv_b_out[l]), h, gate, row(g_post_mix[l]), seq)
        else:
            j = l - n_conv
            w_dkvkr = jnp.concatenate(
                [w_dkv, w_kr, jnp.zeros((d, LANES - QK_ROPE_DIM), F32)], axis=1).astype(BF16)
            qt, k, vt = _proj(h, pos, inv_col, kv_shift, kv_scale, shift, scale, row(g_kv), row(g_pre_mix[l]),
                              w_dkvkr, row(g_ckv), w_uk.astype(BF16), w_uv.T.astype(BF16),
                              w_dq[j].astype(BF16), row(g_cq[j]), w_uq[j].T.astype(BF16), q_scale, batch, seq)
            a = _attention(qt, k, vt)
            h = _attn_out(a.reshape(t, -1), w_o[j], h, gate, row(g_post_mix[l]), seq)
        shift, scale, gate = split(ada_mlp[l], 3)
        h = _mlp(h, shift, scale, gate, row(g_pre_mlp[l]), row(g_post_mlp[l]), mlp_w_up, mlp_w_down, l, seq)
    return h.reshape(batch, seq, d)
```
